```python
import math
import jax, jax.numpy as jnp
from jax import lax
import numpy as np

D_MODEL = 1024
BATCH = 4
SEQ = 8192
DEPTH = 1
DEC_BATCH = 8
DEC_SEQ = 32
PAST_LEN = 4096

CHUNK = 64
GDN_HEADS = 4
GDN_DK = 128
GDN_DV = 128
GDN_CONV = 4
DIFF_HEADS = 4
DIFF_DH = 64
DIFF_DV = 2 * DIFF_DH
ROPE_THETA = 10000.0
Q_BLOCK = 128
D_FF = 2816
FFN_CONV = 3
LN_EPS = 1e-5
RMS_EPS = 1e-6
NEG_INF = -1e30

GDN_QK = GDN_HEADS * GDN_DK
GDN_V = GDN_HEADS * GDN_DV
GDN_QKV = 2 * GDN_QK + GDN_V
DIFF_QK = DIFF_HEADS * 2 * DIFF_DH
DIFF_V = DIFF_HEADS * DIFF_DV
D_IN = GDN_QKV + GDN_V + 2 * GDN_HEADS + 2 * DIFF_QK + DIFF_V
D_MIX = GDN_V + DIFF_V
DEEPNORM_ALPHA = (2 * DEPTH) ** 0.25
DEEPNORM_BETA = (8 * DEPTH) ** -0.25

kernel_name = "hybrid_gdn_diffattn_convffn_stream_step"


def layer_norm(x, g, b):
    xf = x.astype(jnp.float32)
    mu = jnp.mean(xf, -1, keepdims=True)
    var = jnp.mean(jnp.square(xf - mu), -1, keepdims=True)
    return ((xf - mu) * lax.rsqrt(var + LN_EPS) * g + b).astype(x.dtype)


def rms_norm(x, w):
    xf = x.astype(jnp.float32)
    return (xf * lax.rsqrt(jnp.mean(xf * xf, -1, keepdims=True) + RMS_EPS) * w).astype(x.dtype)


def l2norm(x):
    xf = x.astype(jnp.float32)
    return (xf * lax.rsqrt(jnp.sum(xf * xf, -1, keepdims=True) + 1e-6)).astype(x.dtype)


def causal_dwconv(x, buf, w):
    width = w.shape[0]
    T = x.shape[1]
    xp = jnp.concatenate([buf.astype(x.dtype), x], axis=1)
    y = sum(xp[:, j:j + T] * w[j] for j in range(width))
    return y, xp[:, T:]


def rope(x, pos):
    half = DIFF_DH // 2
    inv = ROPE_THETA ** (-jnp.arange(half, dtype=jnp.float32) * (2.0 / DIFF_DH))
    ang = pos.astype(jnp.float32)[:, None] * inv[None, :]
    cos = jnp.cos(ang)[:, None, None, :]
    sin = jnp.sin(ang)[:, None, None, :]
    xf = x.astype(jnp.float32)
    x1, x2 = xf[..., :half], xf[..., half:]
    return jnp.concatenate([x1 * cos - x2 * sin, x2 * cos + x1 * sin], -1).astype(x.dtype)


def gdn_chunked(q, k, v, g, beta, s0, chunk):
    f32 = jnp.float32
    B, T, H, Dk = q.shape
    Dv = v.shape[-1]
    N = T // chunk

    def split(t):
        t = t.astype(f32).reshape(B, N, chunk, H, *t.shape[3:])
        return jnp.moveaxis(t, (1, 3), (0, 2))

    qc = split(q) * (Dk ** -0.5)
    kc = split(k)
    vc = split(v)
    bc = split(beta)
    gc = jnp.cumsum(split(g), axis=-1)
    idx = jnp.arange(chunk)
    incl = idx[:, None] >= idx[None, :]
    decay = jnp.exp(jnp.where(incl, gc[..., :, None] - gc[..., None, :], -jnp.inf))
    kb = kc * bc[..., None]
    a_low = jnp.where(idx[:, None] > idx[None, :],
                      jnp.einsum('nbhid,nbhjd->nbhij', kb, kc) * decay, 0.0)
    eye = jnp.eye(chunk, dtype=f32)
    t_inv = lax.linalg.triangular_solve(eye + a_low, jnp.broadcast_to(eye, a_low.shape),
                                        left_side=True, lower=True, unit_diagonal=True)
    u = t_inv @ (vc * bc[..., None])
    w = t_inv @ (kb * jnp.exp(gc)[..., None])
    qk = jnp.einsum('nbhid,nbhjd->nbhij', qc, kc) * decay

    def step(S, xs):
        q_i, k_i, u_i, w_i, g_i, qk_i = xs
        v_new = u_i - w_i @ S
        o_i = (q_i * jnp.exp(g_i)[..., None]) @ S + qk_i @ v_new
        g_last = g_i[..., -1:]
        S = S * jnp.exp(g_last)[..., None] + jnp.einsum(
            'bhcd,bhce->bhde', k_i * jnp.exp(g_last - g_i)[..., None], v_new)
        return S, o_i

    S, o = lax.scan(step, s0.astype(f32), (qc, kc, u, w, gc, qk))
    o = jnp.moveaxis(o, (0, 2), (1, 3)).reshape(B, T, H, Dv)
    return o.astype(v.dtype), S.astype(s0.dtype)


def diff_attention(q, k, v, q_pos, k_pos, lam):
    B, Tq = q.shape[:2]
    nqb = max(Tq // Q_BLOCK, 1)
    bq = Tq // nqb
    qb = jnp.moveaxis(q.reshape(B, nqb, bq, *q.shape[2:]), 1, 0)
    pb = q_pos.reshape(nqb, bq)
    k_chunk = k_pos // CHUNK
    scale = DIFF_DH ** -0.5

    def block(args):
        qi, pi = args
        mask = k_chunk[None, :] <= (pi // CHUNK)[:, None]
        s = jnp.einsum('bqhmd,bkhmd->bhmqk', qi, k).astype(jnp.float32) * scale
        p = jax.nn.softmax(jnp.where(mask, s, NEG_INF), axis=-1)
        a = p[:, :, 0] - lam * p[:, :, 1]
        return jnp.einsum('bhqk,bkhd->bqhd', a.astype(v.dtype), v)

    o = lax.map(block, (qb, pb))
    return jnp.moveaxis(o, 0, 1).reshape(B, Tq, *o.shape[3:])


def trunk_layer(x, pos, k_past, v_past, s0, conv_qkv_buf, conv_ffn_buf, p, lam_init):
    f32 = jnp.float32
    B, T, _ = x.shape
    h = x @ p["w_in"]
    o1 = GDN_QKV
    o2 = o1 + GDN_V
    o3 = o2 + GDN_HEADS
    o4 = o3 + GDN_HEADS
    o5 = o4 + DIFF_QK
    o6 = o5 + DIFF_QK
    qkv_a, gate_a, a_a, b_a = h[..., :o1], h[..., o1:o2], h[..., o2:o3], h[..., o3:o4]
    q_b, k_b, v_b = h[..., o4:o5], h[..., o5:o6], h[..., o6:]

    qkv_c, new_conv_qkv = causal_dwconv(qkv_a, conv_qkv_buf, p["gdn_conv_w"])
    qkv_c = jax.nn.silu(qkv_c)
    qa = l2norm(qkv_c[..., :GDN_QK].reshape(B, T, GDN_HEADS, GDN_DK))
    ka = l2norm(qkv_c[..., GDN_QK:2 * GDN_QK].reshape(B, T, GDN_HEADS, GDN_DK))
    va = qkv_c[..., 2 * GDN_QK:].reshape(B, T, GDN_HEADS, GDN_DV)
    g = -jnp.exp(p["gdn_a_log"].astype(f32)) * jax.nn.softplus(
        a_a.astype(f32) + p["gdn_dt_bias"].astype(f32))
    beta = jax.nn.sigmoid(b_a.astype(f32))
    chunk = CHUNK if T % CHUNK == 0 else T
    oa, s_new = gdn_chunked(qa, ka, va, g, beta, s0, chunk)
    oa = rms_norm(oa, p["gdn_norm_w"]) * jax.nn.silu(gate_a.reshape(B, T, GDN_HEADS, GDN_DV))

    qd = rope(q_b.reshape(B, T, DIFF_HEADS, 2, DIFF_DH), pos)
    kd = rope(k_b.reshape(B, T, DIFF_HEADS, 2, DIFF_DH), pos)
    vd = v_b.reshape(B, T, DIFF_HEADS, DIFF_DV)
    new_k = kd.reshape(B, T, DIFF_HEADS, 2 * DIFF_DH)
    if k_past is None:
        k_all, v_all, k_pos = kd, vd, pos
    else:
        past = k_past.shape[1]
        k_all = jnp.concatenate(
            [k_past.reshape(B, past, DIFF_HEADS, 2, DIFF_DH).astype(kd.dtype), kd], axis=1)
        v_all = jnp.concatenate([v_past.astype(vd.dtype), vd], axis=1)
        k_pos = jnp.concatenate([jnp.arange(past, dtype=pos.dtype), pos])
    lv = p["diff_lambda"].astype(f32)
    lam = jnp.exp(jnp.sum(lv[0] * lv[1])) - jnp.exp(jnp.sum(lv[2] * lv[3])) + lam_init
    ob = diff_attention(qd, k_all, v_all, pos, k_pos, lam)
    ob = rms_norm(ob, p["diff_subln_w"]) * (1.0 - lam_init)

    mix = jnp.concatenate([oa.reshape(B, T, GDN_V), ob.reshape(B, T, DIFF_V).astype(oa.dtype)],
                          axis=-1) @ p["w_o"]
    x = layer_norm(DEEPNORM_ALPHA * x + mix, p["ln1_g"], p["ln1_b"])

    u, new_conv_ffn = causal_dwconv(x @ p["w_up"], conv_ffn_buf, p["ffn_conv_w"])
    u = u + p["ffn_conv_b"]
    y = (jax.nn.silu(u[..., :D_FF]) * u[..., D_FF:]) @ p["w_down"]
    x = layer_norm(DEEPNORM_ALPHA * x + y, p["ln2_g"], p["ln2_b"])
    return x, new_k, vd, s_new, new_conv_qkv, new_conv_ffn


def setup_inputs(seed: int = 0) -> dict:
    key = jax.random.key(seed)
    ks = jax.random.split(key, 26)
    f32 = jnp.float32

    def nrm(k, shape, s):
        return jax.random.normal(k, shape, f32) * s

    L = DEPTH
    dt = jnp.exp(jax.random.uniform(ks[10], (L, GDN_HEADS), f32, math.log(1e-3), math.log(1e-1)))
    return {
        "x_prompt": nrm(ks[0], (BATCH, SEQ, D_MODEL), 1.0),
        "x_sample": nrm(ks[1], (DEC_BATCH, DEC_SEQ, D_MODEL), 1.0),
        "cache_k": nrm(ks[2], (L, DEC_BATCH, PAST_LEN, DIFF_HEADS, 2 * DIFF_DH), 1.0),
        "cache_v": nrm(ks[3], (L, DEC_BATCH, PAST_LEN, DIFF_HEADS, DIFF_DV), 1.0),
        "state_gdn": nrm(ks[4], (L, DEC_BATCH, GDN_HEADS, GDN_DK, GDN_DV), 0.5),
        "state_conv_qkv": nrm(ks[5], (L, DEC_BATCH, GDN_CONV - 1, GDN_QKV), 1.0),
        "state_conv_ffn": nrm(ks[6], (L, DEC_BATCH, FFN_CONV - 1, 2 * D_FF), 1.0),
        "w_in": nrm(ks[7], (L, D_MODEL, D_IN), D_MODEL ** -0.5),
        "gdn_conv_w": nrm(ks[8], (L, GDN_CONV, GDN_QKV), GDN_CONV ** -0.5),
        "gdn_a_log": jnp.log(jax.random.uniform(ks[9], (L, GDN_HEADS), f32, 1.0, 16.0)),
        "gdn_dt_bias": dt + jnp.log(-jnp.expm1(-dt)),
        "gdn_norm_w": 1.0 + nrm(ks[11], (L, GDN_DV), 0.02),
        "diff_lambda": nrm(ks[12], (L, 4, DIFF_DH), 0.1),
        "diff_subln_w": 1.0 + nrm(ks[13], (L, DIFF_DV), 0.02),
        "w_o": nrm(ks[14], (L, D_MIX, D_MODEL), D_MIX ** -0.5 * DEEPNORM_BETA),
        "ln1_g": 1.0 + nrm(ks[15], (L, D_MODEL), 0.02),
        "ln1_b": nrm(ks[16], (L, D_MODEL), 0.02),
        "w_up": nrm(ks[17], (L, D_MODEL, 2 * D_FF), D_MODEL ** -0.5),
        "ffn_conv_w": nrm(ks[18], (L, FFN_CONV, 2 * D_FF), FFN_CONV ** -0.5),
        "ffn_conv_b": nrm(ks[19], (L, 2 * D_FF), 0.02),
        "w_down": nrm(ks[20], (L, D_FF, D_MODEL), D_FF ** -0.5 * DEEPNORM_BETA),
        "ln2_g": 1.0 + nrm(ks[21], (L, D_MODEL), 0.02),
        "ln2_b": nrm(ks[22], (L, D_MODEL), 0.02),
    }


def reference(x_prompt, x_sample, cache_k, cache_v, state_gdn, state_conv_qkv, state_conv_ffn,
              w_in, gdn_conv_w, gdn_a_log, gdn_dt_bias, gdn_norm_w, diff_lambda, diff_subln_w,
              w_o, ln1_g, ln1_b, w_up, ffn_conv_w, ffn_conv_b, w_down, ln2_g, ln2_b):
    Bp, Tp, _ = x_prompt.shape
    Ts = x_sample.shape[1]
    past = cache_k.shape[2]
    pos_p = jnp.arange(Tp, dtype=jnp.int32)
    pos_s = past + jnp.arange(Ts, dtype=jnp.int32)
    xp, xs = x_prompt, x_sample
    outs_p, outs_s = [], []
    for l in range(DEPTH):
        p = {"w_in": w_in[l], "gdn_conv_w": gdn_conv_w[l], "gdn_a_log": gdn_a_log[l],
             "gdn_dt_bias": gdn_dt_bias[l], "gdn_norm_w": gdn_norm_w[l],
             "diff_lambda": diff_lambda[l], "diff_subln_w": diff_subln_w[l], "w_o": w_o[l],
             "ln1_g": ln1_g[l], "ln1_b": ln1_b[l], "w_up": w_up[l], "ffn_conv_w": ffn_conv_w[l],
             "ffn_conv_b": ffn_conv_b[l], "w_down": w_down[l], "ln2_g": ln2_g[l], "ln2_b": ln2_b[l]}
        lam_init = 0.8 - 0.6 * math.exp(-0.3 * l)
        s0_p = jnp.zeros((Bp, GDN_HEADS, GDN_DK, GDN_DV), jnp.float32)
        cq0_p = jnp.zeros((Bp, GDN_CONV - 1, GDN_QKV), xp.dtype)
        cf0_p = jnp.zeros((Bp, FFN_CONV - 1, 2 * D_FF), xp.dtype)
        xp, kp, vp, sp, cqp, cfp = trunk_layer(xp, pos_p, None, None, s0_p, cq0_p, cf0_p, p, lam_init)
        xs, ksn, vsn, ssn, cqs, cfs = trunk_layer(xs, pos_s, cache_k[l], cache_v[l], state_gdn[l],
                                                  state_conv_qkv[l], state_conv_ffn[l], p, lam_init)
        outs_p.append((kp, vp, sp, cqp, cfp))
        outs_s.append((ksn, vsn, ssn, cqs, cfs))

    def stk(outs, i):
        return jnp.stack([o[i] for o in outs])

    return (xp, xs,
            stk(outs_p, 0), stk(outs_p, 1), stk(outs_p, 2), stk(outs_p, 3), stk(outs_p, 4),
            stk(outs_s, 0), stk(outs_s, 1), stk(outs_s, 2), stk(outs_s, 3), stk(outs_s, 4))
```

```python
import functools
import math

import jax
import jax.numpy as jnp
from jax import lax
from jax.experimental import pallas as pl
from jax.experimental.pallas import tpu as pltpu

F32 = jnp.float32
BF16 = jnp.bfloat16

LANES = 128
SUBLANES = 8
VMEM_LIMIT_BYTES = 56 * 1024 * 1024

CHUNK = 64
GDN_HEADS = 4
GDN_DK = 128
GDN_DV = 128
GDN_CONV = 4
DIFF_HEADS = 4
DIFF_DH = 64
DIFF_DV = 2 * DIFF_DH
ROPE_THETA = 10000.0
FFN_CONV = 3
LN_EPS = 1e-5
RMS_EPS = 1e-6
L2_EPS = 1e-6
NEG_INF = -1e30

GDN_QK = GDN_HEADS * GDN_DK
GDN_V = GDN_HEADS * GDN_DV
GDN_QKV = 2 * GDN_QK + GDN_V
DIFF_QK = DIFF_HEADS * 2 * DIFF_DH
DIFF_V = DIFF_HEADS * DIFF_DV

COL_GATE = GDN_QKV
COL_QB = COL_GATE + GDN_V
COL_KB = COL_QB + DIFF_QK
COL_VB = COL_KB + DIFF_QK
COL_AB = COL_VB + DIFF_V
D_IN_PAD = COL_AB + LANES


def _nt_dot(a, b):
    return lax.dot_general(a, b, (((1,), (1,)), ((), ())), preferred_element_type=F32)


def _tn_dot(a, b):
    return lax.dot_general(a, b, (((0,), (0,)), ((), ())), preferred_element_type=F32)


def _dot(a, b):
    return jnp.dot(a, b, preferred_element_type=F32)


def _dot_hi(a, b):
    return jnp.dot(a, b, preferred_element_type=F32, precision=lax.Precision.HIGHEST)


def _sigmoid(x):
    return 1.0 / (1.0 + jnp.exp(-x))


def _silu(x):
    return x * _sigmoid(x)


def _softplus(x):
    return jnp.maximum(x, 0.0) + jnp.log(1.0 + jnp.exp(-jnp.abs(x)))


def _layer_norm(x, g, b):
    mu = jnp.mean(x, axis=-1, keepdims=True)
    xc = x - mu
    var = jnp.mean(xc * xc, axis=-1, keepdims=True)
    return xc * lax.rsqrt(var + LN_EPS) * g + b


def _shifted_rows(prev, cur, back):
    if back == 0:
        return cur
    ext = jnp.concatenate([prev, cur], axis=0)
    n = cur.shape[0]
    return ext[SUBLANES - back:SUBLANES - back + n]


def _inproj_kernel(x_ref, w_ref, convw_ref, cbuf_ref, alog_ref, dtb_ref, cos_ref, sin_ref,
                   qg_ref, kg_ref, vg_ref, gate_ref, gb_ref, qd_ref, kd_ref, vd_ref,
                   kout_ref, vout_ref, cstate_ref, prev_ref):
    t = pl.program_id(1)

    @pl.when(t == 0)
    def _():
        prev_ref[...] = cbuf_ref[0]

    tm = x_ref.shape[1]
    x = x_ref[0].astype(BF16)
    h = _dot(x, w_ref[...])

    qkv = h[:, :GDN_QKV]
    prev = prev_ref[...]
    cw = convw_ref[...]
    y = qkv * cw[GDN_CONV - 1:GDN_CONV]
    for j in range(GDN_CONV - 1):
        y = y + _shifted_rows(prev, qkv, GDN_CONV - 1 - j) * cw[j:j + 1]
    tail = qkv[tm - SUBLANES:]
    prev_ref[...] = tail
    cstate_ref[0] = tail
    y = _silu(y)

    for hd in range(GDN_HEADS):
        for base, ref in ((0, qg_ref), (GDN_QK, kg_ref)):
            z = y[:, base + hd * GDN_DK: base + (hd + 1) * GDN_DK]
            z = z * lax.rsqrt(jnp.sum(z * z, axis=-1, keepdims=True) + L2_EPS)
            ref[0, :, hd * GDN_DK:(hd + 1) * GDN_DK] = z.astype(BF16)
    vg_ref[0] = y[:, 2 * GDN_QK:].astype(BF16)
    gate_ref[0] = _silu(h[:, COL_GATE:COL_GATE + GDN_V]).astype(BF16)

    ab = h[:, COL_AB:COL_AB + LANES]
    g = -jnp.exp(alog_ref[...]) * _softplus(ab + dtb_ref[...])
    lane = lax.broadcasted_iota(jnp.int32, ab.shape, 1)
    gb_ref[0] = jnp.where(lane < GDN_HEADS, g, _sigmoid(ab))

    cos = cos_ref[...]
    sin = sin_ref[...]
    first_half = (lax.broadcasted_iota(jnp.int32, (tm, LANES), 1) % DIFF_DH) < (DIFF_DH // 2)
    scale = DIFF_DH ** -0.5
    for hd in range(DIFF_HEADS):
        sl = slice(hd * LANES, (hd + 1) * LANES)
        for base, is_q in ((COL_QB, True), (COL_KB, False)):
            z = h[:, base + hd * LANES: base + (hd + 1) * LANES]
            swapped = jnp.where(first_half, pltpu.roll(z, LANES - DIFF_DH // 2, 1),
                                pltpu.roll(z, DIFF_DH // 2, 1))
            r = z * cos + swapped * sin
            if is_q:
                qd_ref[0, :, sl] = (r * scale).astype(BF16)
            else:
                kout_ref[0, :, sl] = r
                kd_ref[0, :, sl] = r.astype(BF16)
    vb = h[:, COL_VB:COL_VB + DIFF_V]
    vout_ref[0] = vb
    vd_ref[0] = vb.astype(BF16)


def _inproj(x, w_in_r, conv_w, cbuf8, alog, dtb, cos_t, sin_t, tm):
    B, T, D = x.shape
    nt = T // tm
    tok = lambda width, dt: jax.ShapeDtypeStruct((B, T, width), dt)
    tok_spec = lambda width: pl.BlockSpec((1, tm, width), lambda b, t: (b, t, 0))
    full = lambda a: pl.BlockSpec(a.shape, lambda b, t: (0,) * a.ndim)
    out_shape = (tok(GDN_QK, BF16), tok(GDN_QK, BF16), tok(GDN_V, BF16), tok(GDN_V, BF16),
                 tok(LANES, F32), tok(DIFF_QK, BF16), tok(DIFF_QK, BF16), tok(DIFF_V, BF16),
                 tok(DIFF_QK, F32), tok(DIFF_V, F32),
                 jax.ShapeDtypeStruct((B, SUBLANES, GDN_QKV), F32))
    out_specs = (tok_spec(GDN_QK), tok_spec(GDN_QK), tok_spec(GDN_V), tok_spec(GDN_V),
                 tok_spec(LANES), tok_spec(DIFF_QK), tok_spec(DIFF_QK), tok_spec(DIFF_V),
                 tok_spec(DIFF_QK), tok_spec(DIFF_V),
                 pl.BlockSpec((1, SUBLANES, GDN_QKV), lambda b, t: (b, 0, 0)))
    in_specs = [tok_spec(D), full(w_in_r), full(conv_w),
                pl.BlockSpec((1, SUBLANES, GDN_QKV), lambda b, t: (b, 0, 0)),
                full(alog), full(dtb),
                pl.BlockSpec((tm, LANES), lambda b, t: (t, 0)),
                pl.BlockSpec((tm, LANES), lambda b, t: (t, 0))]
    return pl.pallas_call(
        _inproj_kernel,
        grid=(B, nt),
        in_specs=in_specs,
        out_specs=out_specs,
        out_shape=out_shape,
        scratch_shapes=[pltpu.VMEM((SUBLANES, GDN_QKV), F32)],
        compiler_params=pltpu.CompilerParams(
            dimension_semantics=("arbitrary", "arbitrary"),
            vmem_limit_bytes=VMEM_LIMIT_BYTES),
        name="inproj",
    )(x, w_in_r, conv_w, cbuf8, alog, dtb, cos_t, sin_t)


def _gdn_kernel(q_ref, k_ref, v_ref, gate_ref, gb_ref, s0_ref, nw_ref,
                o_ref, sout_ref, s_ref, *, chunk, n_chunks):
    t = pl.program_id(1)

    @pl.when(t == 0)
    def _():
        s_ref[...] = s0_ref[0]

    C = chunk
    row = lax.broadcasted_iota(jnp.int32, (C, C), 0)
    col = lax.broadcasted_iota(jnp.int32, (C, C), 1)
    incl = row >= col
    strict = row > col
    tri = incl.astype(F32)
    eye = (row == col).astype(F32)
    scale = GDN_DK ** -0.5
    nw = nw_ref[...]

    for c in range(n_chunks):
        rows = slice(c * C, (c + 1) * C)
        gbc = gb_ref[0, rows, :]
        gc_all = _dot_hi(tri, gbc)
        gc_t = gc_all.T
        for hd in range(GDN_HEADS):
            cols = slice(hd * GDN_DK, (hd + 1) * GDN_DK)
            q = q_ref[0, rows, cols].astype(F32) * scale
            k = k_ref[0, rows, cols].astype(F32)
            v = v_ref[0, rows, cols].astype(F32)
            gcol = gc_all[:, hd:hd + 1]
            grow = gc_t[hd:hd + 1, :]
            bcol = gbc[:, GDN_HEADS + hd:GDN_HEADS + hd + 1]
            decay = jnp.exp(jnp.where(incl, gcol - grow, -jnp.inf))
            kb = k * bcol
            kbf = k.astype(BF16)
            a_low = jnp.where(strict, _nt_dot(kb.astype(BF16), kbf) * decay, 0.0)
            qk = _nt_dot(q.astype(BF16), kbf) * decay
            p = -a_low
            t_inv = eye + p
            n_sq = max(int(math.ceil(math.log2(C))) - 1, 0)
            for _ in range(n_sq):
                p = _dot_hi(p, p)
                t_inv = t_inv + _dot_hi(t_inv, p)
            t_bf = t_inv.astype(BF16)
            eg = jnp.exp(gcol)
            u = _dot(t_bf, (v * bcol).astype(BF16))
            w = _dot(t_bf, (kb * eg).astype(BF16))
            s = s_ref[hd]
            s_bf = s.astype(BF16)
            v_new = u - _dot(w.astype(BF16), s_bf)
            v_new_bf = v_new.astype(BF16)
            o = _dot((q * eg).astype(BF16), s_bf) + _dot(qk.astype(BF16), v_new_bf)
            g_last = gcol[C - 1:C, :]
            s_ref[hd] = s * jnp.exp(g_last) + _tn_dot((k * jnp.exp(g_last - gcol)).astype(BF16),
                                                      v_new_bf)
            on = o * lax.rsqrt(jnp.mean(o * o, axis=-1, keepdims=True) + RMS_EPS) * nw
            o_ref[0, rows, cols] = (on * gate_ref[0, rows, cols].astype(F32)).astype(BF16)

    sout_ref[0] = s_ref[...]


def _gdn(qg, kg, vg, gate, gb, s0, norm_w, chunk, n_chunks):
    B, T, _ = qg.shape
    tb = chunk * n_chunks
    nt = T // tb
    tok_spec = lambda width: pl.BlockSpec((1, tb, width), lambda b, t: (b, t, 0))
    st_spec = pl.BlockSpec((1, GDN_HEADS, GDN_DK, GDN_DV), lambda b, t: (b, 0, 0, 0))
    return pl.pallas_call(
        functools.partial(_gdn_kernel, chunk=chunk, n_chunks=n_chunks),
        grid=(B, nt),
        in_specs=[tok_spec(GDN_QK), tok_spec(GDN_QK), tok_spec(GDN_V), tok_spec(GDN_V),
                  tok_spec(LANES), st_spec, pl.BlockSpec((1, GDN_DV), lambda b, t: (0, 0))],
        out_specs=(tok_spec(GDN_V), st_spec),
        out_shape=(jax.ShapeDtypeStruct((B, T, GDN_V), BF16),
                   jax.ShapeDtypeStruct((B, GDN_HEADS, GDN_DK, GDN_DV), F32)),
        scratch_shapes=[pltpu.VMEM((GDN_HEADS, GDN_DK, GDN_DV), F32)],
        compiler_params=pltpu.CompilerParams(
            dimension_semantics=("arbitrary", "arbitrary"),
            vmem_limit_bytes=VMEM_LIMIT_BYTES),
        name="gdn",
    )(qg, kg, vg, gate, gb, s0, norm_w)


def _attn_kernel(q_ref, k_ref, v_ref, lam_ref, w_ref, o_ref, *, bq, bk, q_pos0, lam_init):
    qi = pl.program_id(2)
    tk = k_ref.shape[1]
    n_blocks = tk // bk

    q = q_ref[0]
    lane = lax.broadcasted_iota(jnp.int32, q.shape, 1)
    zero = jnp.zeros_like(q)
    qs = jnp.concatenate([jnp.where(lane < DIFF_DH, q, zero),
                          jnp.where(lane >= DIFF_DH, q, zero)], axis=0)

    q_first = q_pos0 + qi * bq
    q_last = q_first + (bq - 1)
    vis_first = (q_first // CHUNK + 1) * CHUNK
    vis_last = (q_last // CHUNK + 1) * CHUNK
    n_full = jnp.minimum(vis_first // bk, n_blocks)
    n_all = jnp.minimum((vis_last + bk - 1) // bk, n_blocks)

    def step(kv, carry, masked):
        m, l, acc = carry
        start = pl.multiple_of(kv * bk, bk)
        k = k_ref[0, pl.ds(start, bk), :].astype(BF16)
        v = v_ref[0, pl.ds(start, bk), :].astype(BF16)
        s = _nt_dot(qs, k)
        if masked:
            kpos = start + lax.broadcasted_iota(jnp.int32, (2 * bq, bk), 1)
            r = lax.broadcasted_iota(jnp.int32, (2 * bq, bk), 0)
            qpos = q_first + jnp.where(r >= bq, r - bq, r)
            s = jnp.where(kpos // CHUNK <= qpos // CHUNK, s, NEG_INF)
        m_new = jnp.maximum(m, jnp.max(s, axis=-1, keepdims=True))
        alpha = jnp.exp(m - m_new)
        p = jnp.exp(s - m_new)
        l = alpha * l + jnp.sum(p, axis=-1, keepdims=True)
        acc = alpha * acc + _dot(p.astype(BF16), v)
        return m_new, l, acc

    carry = (jnp.full((2 * bq, 1), NEG_INF, F32), jnp.zeros((2 * bq, 1), F32),
             jnp.zeros((2 * bq, DIFF_DV), F32))
    carry = lax.fori_loop(0, n_full, functools.partial(step, masked=False), carry)
    carry = lax.fori_loop(n_full, n_all, functools.partial(step, masked=True), carry)
    _, l, acc = carry

    lv = lam_ref[...]
    lam = (jnp.exp(jnp.sum(lv[0:1] * lv[1:2], axis=-1, keepdims=True))
           - jnp.exp(jnp.sum(lv[2:3] * lv[3:4], axis=-1, keepdims=True)) + lam_init)
    o = acc / l
    out = o[:bq] - lam * o[bq:]
    out = out * lax.rsqrt(jnp.mean(out * out, axis=-1, keepdims=True) + RMS_EPS) * w_ref[...]
    o_ref[0] = (out * (1.0 - lam_init)).astype(BF16)


def _attn(qd, k_all, v_all, lam_p, subln_w, bq, bk, q_pos0, lam_init):
    B, Tq, _ = qd.shape
    Tk = k_all.shape[1]
    nq = Tq // bq
    return pl.pallas_call(
        functools.partial(_attn_kernel, bq=bq, bk=bk, q_pos0=q_pos0, lam_init=lam_init),
        grid=(B, DIFF_HEADS, nq),
        in_specs=[pl.BlockSpec((1, bq, LANES), lambda b, h, i: (b, i, h)),
                  pl.BlockSpec((1, Tk, LANES), lambda b, h, i: (b, 0, h)),
                  pl.BlockSpec((1, Tk, LANES), lambda b, h, i: (b, 0, h)),
                  pl.BlockSpec(lam_p.shape, lambda b, h, i: (0, 0)),
                  pl.BlockSpec(subln_w.shape, lambda b, h, i: (0, 0))],
        out_specs=pl.BlockSpec((1, bq, LANES), lambda b, h, i: (b, i, h)),
        out_shape=jax.ShapeDtypeStruct((B, Tq, DIFF_V), BF16),
        compiler_params=pltpu.CompilerParams(
            dimension_semantics=("arbitrary", "arbitrary", "arbitrary"),
            vmem_limit_bytes=VMEM_LIMIT_BYTES),
        name="diffattn",
    )(qd, k_all, v_all, lam_p, subln_w)


def _ffn_kernel(x_ref, oa_ref, ob_ref, wo_ref, g1_ref, b1_ref, wup_ref, cw_ref, cb_ref, cbuf_ref,
                wdn_ref, g2_ref, b2_ref, y_ref, cstate_ref, prev_ref, *, alpha, d_ff):
    t = pl.program_id(1)

    @pl.when(t == 0)
    def _():
        prev_ref[...] = cbuf_ref[0]

    tm = x_ref.shape[1]
    mix = _dot(oa_ref[0], wo_ref[:GDN_V, :]) + _dot(ob_ref[0], wo_ref[GDN_V:, :])
    x1 = _layer_norm(alpha * x_ref[0] + mix, g1_ref[...], b1_ref[...])
    up = _dot(x1.astype(BF16), wup_ref[...])
    prev = prev_ref[...]
    cw = cw_ref[...]
    u = up * cw[FFN_CONV - 1:FFN_CONV] + cb_ref[...]
    for j in range(FFN_CONV - 1):
        u = u + _shifted_rows(prev, up, FFN_CONV - 1 - j) * cw[j:j + 1]
    tail = up[tm - SUBLANES:]
    prev_ref[...] = tail
    cstate_ref[0] = tail
    hh = _silu(u[:, :d_ff]) * u[:, d_ff:]
    y = _dot(hh.astype(BF16), wdn_ref[...])
    y_ref[0] = _layer_norm(alpha * x1 + y, g2_ref[...], b2_ref[...])


def _ffn(x, oa, ob, w_o, g1, b1, w_up, cw, cb, cbuf8, w_dn, g2, b2, tm, alpha):
    B, T, D = x.shape
    nt = T // tm
    d_ff = w_dn.shape[0]
    tok_spec = lambda width: pl.BlockSpec((1, tm, width), lambda b, t: (b, t, 0))
    full = lambda a: pl.BlockSpec(a.shape, lambda b, t: (0,) * a.ndim)
    st_spec = pl.BlockSpec((1, SUBLANES, 2 * d_ff), lambda b, t: (b, 0, 0))
    return pl.pallas_call(
        functools.partial(_ffn_kernel, alpha=alpha, d_ff=d_ff),
        grid=(B, nt),
        in_specs=[tok_spec(D), tok_spec(GDN_V), tok_spec(DIFF_V), full(w_o), full(g1), full(b1),
                  full(w_up), full(cw), full(cb), st_spec, full(w_dn), full(g2), full(b2)],
        out_specs=(tok_spec(D), st_spec),
        out_shape=(jax.ShapeDtypeStruct((B, T, D), F32),
                   jax.ShapeDtypeStruct((B, SUBLANES, 2 * d_ff), F32)),
        scratch_shapes=[pltpu.VMEM((SUBLANES, 2 * d_ff), F32)],
        compiler_params=pltpu.CompilerParams(
            dimension_semantics=("arbitrary", "arbitrary"),
            vmem_limit_bytes=VMEM_LIMIT_BYTES),
        name="ffn",
    )(x, oa, ob, w_o, g1, b1, w_up, cw, cb, cbuf8, w_dn, g2, b2)


def _pad_rows_front(buf, rows):
    B, r, C = buf.shape
    return jnp.concatenate([jnp.zeros((B, rows - r, C), buf.dtype), buf], axis=1)


def _lane_row(vec):
    n = vec.shape[0]
    pad = (-n) % LANES
    return jnp.pad(vec.astype(F32), (0, pad)).reshape(1, n + pad)


def _rope_tables(pos0, T):
    half = DIFF_DH // 2
    inv = ROPE_THETA ** (-jnp.arange(half, dtype=F32) * (2.0 / DIFF_DH))
    pos = pos0 + jnp.arange(T, dtype=jnp.int32)
    ang = pos.astype(F32)[:, None] * inv[None, :]
    cos = jnp.cos(ang)
    sin = jnp.sin(ang)
    reps = LANES // DIFF_DH
    cos_t = jnp.tile(jnp.concatenate([cos, cos], axis=-1), (1, reps))
    sin_t = jnp.tile(jnp.concatenate([-sin, sin], axis=-1), (1, reps))
    return cos_t, sin_t


def _largest_divisor(n, cap, mult):
    best = None
    for d in range(mult, min(n, cap) + 1, mult):
        if n % d == 0:
            best = d
    return best if best is not None else n


def _layer(x, pos0, k_past, v_past, s0, conv_qkv_buf, conv_ffn_buf, p, lam_init, alpha):
    B, T, _ = x.shape
    tm = _largest_divisor(T, 512, SUBLANES)
    cos_t, sin_t = _rope_tables(pos0, T)
    (qg, kg, vg, gate, gb, qd, kd, vd, k_new, v_new, cq8) = _inproj(
        x, p["w_in"], p["gdn_conv_w"], _pad_rows_front(conv_qkv_buf, SUBLANES),
        p["alog"], p["dtb"], cos_t, sin_t, tm)

    chunk = CHUNK if T % CHUNK == 0 else T
    n_chunks = 2 if (T // chunk) % 2 == 0 else 1
    oa, s_new = _gdn(qg, kg, vg, gate, gb, s0, p["gdn_norm_w"], chunk, n_chunks)

    if k_past is None:
        k_all, v_all = kd, vd
    else:
        k_all = jnp.concatenate([k_past.reshape(B, -1, DIFF_QK).astype(BF16), kd], axis=1)
        v_all = jnp.concatenate([v_past.reshape(B, -1, DIFF_V).astype(BF16), vd], axis=1)
    Tk = k_all.shape[1]
    bq = _largest_divisor(T, 256, 16)
    bk = _largest_divisor(Tk, 1536 if Tk % 512 else 512, 16)
    ob = _attn(qd, k_all, v_all, p["diff_lambda"], p["diff_subln_w"], bq, bk, pos0, lam_init)

    tm_f = _largest_divisor(T, 256, SUBLANES)
    y, cf8 = _ffn(x, oa, ob, p["w_o"], p["ln1_g"], p["ln1_b"], p["w_up"], p["ffn_conv_w"],
                  p["ffn_conv_b"], _pad_rows_front(conv_ffn_buf, SUBLANES), p["w_down"],
                  p["ln2_g"], p["ln2_b"], tm_f, alpha)

    new_k = k_new.reshape(B, T, DIFF_HEADS, 2 * DIFF_DH)
    new_v = v_new.reshape(B, T, DIFF_HEADS, DIFF_DV)
    return (y, new_k, new_v, s_new, cq8[:, SUBLANES - (GDN_CONV - 1):],
            cf8[:, SUBLANES - (FFN_CONV - 1):])


def _prep_params(l, w_in, gdn_conv_w, gdn_a_log, gdn_dt_bias, gdn_norm_w, diff_lambda,
                 diff_subln_w, w_o, ln1_g, ln1_b, w_up, ffn_conv_w, ffn_conv_b, w_down,
                 ln2_g, ln2_b):
    o1 = GDN_QKV
    o2 = o1 + GDN_V
    o3 = o2 + GDN_HEADS
    o4 = o3 + GDN_HEADS
    w = w_in[l]
    d_model = w.shape[0]
    w_r = jnp.concatenate(
        [w[:, :o2], w[:, o4:], w[:, o2:o4],
         jnp.zeros((d_model, LANES - 2 * GDN_HEADS), w.dtype)], axis=1).astype(BF16)
    row = lambda v: v.astype(F32).reshape(1, -1)
    return {
        "w_in": w_r, "gdn_conv_w": gdn_conv_w[l].astype(F32),
        "alog": _lane_row(gdn_a_log[l]), "dtb": _lane_row(gdn_dt_bias[l]),
        "gdn_norm_w": row(gdn_norm_w[l]), "diff_lambda": diff_lambda[l].astype(F32),
        "diff_subln_w": row(diff_subln_w[l]), "w_o": w_o[l].astype(BF16),
        "ln1_g": row(ln1_g[l]), "ln1_b": row(ln1_b[l]), "w_up": w_up[l].astype(BF16),
        "ffn_conv_w": ffn_conv_w[l].astype(F32), "ffn_conv_b": row(ffn_conv_b[l]),
        "w_down": w_down[l].astype(BF16), "ln2_g": row(ln2_g[l]), "ln2_b": row(ln2_b[l]),
    }


def kernel(x_prompt, x_sample, cache_k, cache_v, state_gdn, state_conv_qkv, state_conv_ffn, w_in, gdn_conv_w, gdn_a_log, gdn_dt_bias, gdn_norm_w, diff_lambda, diff_subln_w, w_o, ln1_g, ln1_b, w_up, ffn_conv_w, ffn_conv_b, w_down, ln2_g, ln2_b):
    depth = w_in.shape[0]
    Bp = x_prompt.shape[0]
    past = cache_k.shape[2]
    alpha = (2 * depth) ** 0.25
    xp, xs = x_prompt, x_sample
    outs_p, outs_s = [], []
    for l in range(depth):
        p = _prep_params(l, w_in, gdn_conv_w, gdn_a_log, gdn_dt_bias, gdn_norm_w, diff_lambda,
                         diff_subln_w, w_o, ln1_g, ln1_b, w_up, ffn_conv_w, ffn_conv_b, w_down,
                         ln2_g, ln2_b)
        lam_init = 0.8 - 0.6 * math.exp(-0.3 * l)
        s0_p = jnp.zeros((Bp, GDN_HEADS, GDN_DK, GDN_DV), F32)
        cq0_p = jnp.zeros((Bp, GDN_CONV - 1, GDN_QKV), xp.dtype)
        cf0_p = jnp.zeros((Bp, FFN_CONV - 1, state_conv_ffn.shape[-1]), xp.dtype)
        xp, kp, vp, sp, cqp, cfp = _layer(xp, 0, None, None, s0_p, cq0_p, cf0_p, p, lam_init, alpha)
        xs, ksn, vsn, ssn, cqs, cfs = _layer(xs, past, cache_k[l], cache_v[l], state_gdn[l],
                                             state_conv_qkv[l], state_conv_ffn[l], p, lam_init,
                                             alpha)
        outs_p.append((kp, vp, sp, cqp, cfp))
        outs_s.append((ksn, vsn, ssn, cqs, cfs))

    def stk(outs, i):
        return jnp.stack([o[i] for o in outs])

    return (xp, xs,
            stk(outs_p, 0), stk(outs_p, 1), stk(outs_p, 2), stk(outs_p, 3), stk(outs_p, 4),
            stk(outs_s, 0), stk(outs_s, 1), stk(outs_s, 2), stk(outs_s, 3), stk(outs_s, 4))
```

```python
import functools
import math

import jax
import jax.numpy as jnp
from jax import lax
from jax.experimental import pallas as pl
from jax.experimental.pallas import tpu as pltpu

F32 = jnp.float32
BF16 = jnp.bfloat16

LANES = 128
SUBLANES = 8
VMEM_LIMIT_BYTES = 56 * 1024 * 1024

CHUNK = 64
GDN_HEADS = 4
GDN_DK = 128
GDN_DV = 128
GDN_CONV = 4
DIFF_HEADS = 4
DIFF_DH = 64
DIFF_DV = 2 * DIFF_DH
ROPE_THETA = 10000.0
FFN_CONV = 3
LN_EPS = 1e-5
RMS_EPS = 1e-6
L2_EPS = 1e-6
NEG_INF = -1e30

GDN_QK = GDN_HEADS * GDN_DK
GDN_V = GDN_HEADS * GDN_DV
GDN_QKV = 2 * GDN_QK + GDN_V
DIFF_QK = DIFF_HEADS * 2 * DIFF_DH
DIFF_V = DIFF_HEADS * DIFF_DV

COL_GATE = GDN_QKV
COL_QB = COL_GATE + GDN_V
COL_KB = COL_QB + DIFF_QK
COL_VB = COL_KB + DIFF_QK
COL_AB = COL_VB + DIFF_V
D_IN_PAD = COL_AB + LANES


def _nt_dot(a, b):
    return lax.dot_general(a, b, (((1,), (1,)), ((), ())), preferred_element_type=F32)


def _tn_dot(a, b):
    return lax.dot_general(a, b, (((0,), (0,)), ((), ())), preferred_element_type=F32)


def _dot(a, b):
    return jnp.dot(a, b, preferred_element_type=F32)


def _dot_hi(a, b):
    return jnp.dot(a, b, preferred_element_type=F32, precision=lax.Precision.HIGHEST)


def _sigmoid(x):
    return 1.0 / (1.0 + jnp.exp(-x))


def _silu(x):
    return x * _sigmoid(x)


def _softplus(x):
    return jnp.maximum(x, 0.0) + jnp.log(1.0 + jnp.exp(-jnp.abs(x)))


def _layer_norm(x, g, b):
    mu = jnp.mean(x, axis=-1, keepdims=True)
    xc = x - mu
    var = jnp.mean(xc * xc, axis=-1, keepdims=True)
    return xc * lax.rsqrt(var + LN_EPS) * g + b


def _shifted_rows(prev, cur, back):
    if back == 0:
        return cur
    ext = jnp.concatenate([prev, cur], axis=0)
    n = cur.shape[0]
    return ext[SUBLANES - back:SUBLANES - back + n]


def _inproj_kernel(x_ref, w_ref, convw_ref, cbuf_ref, alog_ref, dtb_ref, cos_ref, sin_ref,
                   qg_ref, kg_ref, vg_ref, gate_ref, gb_ref, qd_ref, kd_ref, vd_ref,
                   kout_ref, vout_ref, cstate_ref, prev_ref):
    t = pl.program_id(1)

    @pl.when(t == 0)
    def _():
        prev_ref[...] = cbuf_ref[0]

    tm = x_ref.shape[1]
    x = x_ref[0].astype(BF16)
    h = _dot(x, w_ref[...])

    qkv = h[:, :GDN_QKV]
    prev = prev_ref[...]
    cw = convw_ref[...]
    y = qkv * cw[GDN_CONV - 1:GDN_CONV]
    for j in range(GDN_CONV - 1):
        y = y + _shifted_rows(prev, qkv, GDN_CONV - 1 - j) * cw[j:j + 1]
    tail = qkv[tm - SUBLANES:]
    prev_ref[...] = tail
    cstate_ref[0] = tail
    y = _silu(y)

    for hd in range(GDN_HEADS):
        for base, ref in ((0, qg_ref), (GDN_QK, kg_ref)):
            z = y[:, base + hd * GDN_DK: base + (hd + 1) * GDN_DK]
            z = z * lax.rsqrt(jnp.sum(z * z, axis=-1, keepdims=True) + L2_EPS)
            ref[0, :, hd * GDN_DK:(hd + 1) * GDN_DK] = z.astype(BF16)
    vg_ref[0] = y[:, 2 * GDN_QK:].astype(BF16)
    gate_ref[0] = _silu(h[:, COL_GATE:COL_GATE + GDN_V]).astype(BF16)

    ab = h[:, COL_AB:COL_AB + LANES]
    g = -jnp.exp(alog_ref[...]) * _softplus(ab + dtb_ref[...])
    lane = lax.broadcasted_iota(jnp.int32, ab.shape, 1)
    gb_ref[0] = jnp.where(lane < GDN_HEADS, g, _sigmoid(ab))

    cos = cos_ref[...]
    sin = sin_ref[...]
    first_half = (lax.broadcasted_iota(jnp.int32, (tm, LANES), 1) % DIFF_DH) < (DIFF_DH // 2)
    scale = DIFF_DH ** -0.5
    for hd in range(DIFF_HEADS):
        sl = slice(hd * LANES, (hd + 1) * LANES)
        for base, is_q in ((COL_QB, True), (COL_KB, False)):
            z = h[:, base + hd * LANES: base + (hd + 1) * LANES]
            swapped = jnp.where(first_half, pltpu.roll(z, LANES - DIFF_DH // 2, 1),
                                pltpu.roll(z, DIFF_DH // 2, 1))
            r = z * cos + swapped * sin
            if is_q:
                qd_ref[0, :, sl] = (r * scale).astype(BF16)
            else:
                kout_ref[0, :, sl] = r
                kd_ref[0, :, sl] = r.astype(BF16)
    vb = h[:, COL_VB:COL_VB + DIFF_V]
    vout_ref[0] = vb
    vd_ref[0] = vb.astype(BF16)


def _inproj(x, w_in_r, conv_w, cbuf8, alog, dtb, cos_t, sin_t, tm):
    B, T, D = x.shape
    nt = T // tm
    tok = lambda width, dt: jax.ShapeDtypeStruct((B, T, width), dt)
    tok_spec = lambda width: pl.BlockSpec((1, tm, width), lambda b, t: (b, t, 0))
    full = lambda a: pl.BlockSpec(a.shape, lambda b, t: (0,) * a.ndim)
    out_shape = (tok(GDN_QK, BF16), tok(GDN_QK, BF16), tok(GDN_V, BF16), tok(GDN_V, BF16),
                 tok(LANES, F32), tok(DIFF_QK, BF16), tok(DIFF_QK, BF16), tok(DIFF_V, BF16),
                 tok(DIFF_QK, F32), tok(DIFF_V, F32),
                 jax.ShapeDtypeStruct((B, SUBLANES, GDN_QKV), F32))
    out_specs = (tok_spec(GDN_QK), tok_spec(GDN_QK), tok_spec(GDN_V), tok_spec(GDN_V),
                 tok_spec(LANES), tok_spec(DIFF_QK), tok_spec(DIFF_QK), tok_spec(DIFF_V),
                 tok_spec(DIFF_QK), tok_spec(DIFF_V),
                 pl.BlockSpec((1, SUBLANES, GDN_QKV), lambda b, t: (b, 0, 0)))
    in_specs = [tok_spec(D), full(w_in_r), full(conv_w),
                pl.BlockSpec((1, SUBLANES, GDN_QKV), lambda b, t: (b, 0, 0)),
                full(alog), full(dtb),
                pl.BlockSpec((tm, LANES), lambda b, t: (t, 0)),
                pl.BlockSpec((tm, LANES), lambda b, t: (t, 0))]
    return pl.pallas_call(
        _inproj_kernel,
        grid=(B, nt),
        in_specs=in_specs,
        out_specs=out_specs,
        out_shape=out_shape,
        scratch_shapes=[pltpu.VMEM((SUBLANES, GDN_QKV), F32)],
        compiler_params=pltpu.CompilerParams(
            dimension_semantics=("arbitrary", "arbitrary"),
            vmem_limit_bytes=VMEM_LIMIT_BYTES),
        name="inproj",
    )(x, w_in_r, conv_w, cbuf8, alog, dtb, cos_t, sin_t)


def _split_hi_lo(x):
    hi = x.astype(BF16)
    return hi, (x - hi.astype(F32)).astype(BF16)


def _blockdiag_rows(x, width):
    n = x.shape[1] // width
    blk = lax.broadcasted_iota(jnp.int32, x.shape, 1) // width
    zero = jnp.zeros_like(x)
    return jnp.concatenate([jnp.where(blk == j, x, zero) for j in range(n)], axis=0)


def _packed_dot_split(l2, r2, width):
    l_hi, l_lo = _split_hi_lo(l2)
    r_hi, r_lo = _split_hi_lo(r2)
    bd_hi = _blockdiag_rows(r_hi, width)
    bd_lo = _blockdiag_rows(r_lo, width)
    lhs = jnp.concatenate([l_hi, l_lo, l_hi, l_lo], axis=1)
    rhs = jnp.concatenate([bd_hi, bd_hi, bd_lo, bd_lo], axis=0)
    return _dot(lhs, rhs)


def _per_head(cols, lane_head):
    out = cols[0]
    for j in range(1, len(cols)):
        out = jnp.where(lane_head >= j, cols[j], out)
    return out


def _gdn_kernel(q_ref, k_ref, v_ref, gate_ref, gb_ref, s0_ref, nw_ref,
                o_ref, sout_ref, s_ref, *, chunk, n_chunks):
    t = pl.program_id(1)

    @pl.when(t == 0)
    def _():
        s_ref[...] = s0_ref[0]

    C = chunk
    G = LANES // C
    W = G * GDN_DK
    n_groups = GDN_HEADS // G
    row = lax.broadcasted_iota(jnp.int32, (C, LANES), 0)
    lane = lax.broadcasted_iota(jnp.int32, (C, LANES), 1)
    lane_in = lane % C
    lane_head_p = lane // C
    lane_head_w = lax.broadcasted_iota(jnp.int32, (C, W), 1) // GDN_DK
    incl = row >= lane_in
    strict = row > lane_in
    eye_b = row == lane_in
    eye = eye_b.astype(F32)
    r1 = lax.broadcasted_iota(jnp.int32, (C, C), 0)
    c1 = lax.broadcasted_iota(jnp.int32, (C, C), 1)
    tri = (r1 >= c1).astype(F32)
    scale = GDN_DK ** -0.5
    nw = nw_ref[...]
    n_sq = max(int(math.ceil(math.log2(C))) - 1, 0)

    items = []
    for c in range(n_chunks):
        rows = slice(c * C, (c + 1) * C)
        gbc = gb_ref[0, rows, :]
        gc_all = _dot_hi(tri, gbc)
        for gi in range(n_groups):
            heads = [gi * G + j for j in range(G)]
            cols = slice(gi * W, (gi + 1) * W)
            gcols = [gc_all[:, h:h + 1] for h in heads]
            bcols = [gbc[:, GDN_HEADS + h:GDN_HEADS + h + 1] for h in heads]
            gcol_p = _per_head(gcols, lane_head_p)
            grow_p = jnp.sum(jnp.where(eye_b, gcol_p, 0.0), axis=0, keepdims=True)
            decay = jnp.exp(jnp.where(incl, gcol_p - grow_p, -jnp.inf))
            b_w = _per_head(bcols, lane_head_w)
            eg_w = _per_head([jnp.exp(g) for g in gcols], lane_head_w)
            q = q_ref[0, rows, cols].astype(F32) * scale
            k = k_ref[0, rows, cols].astype(F32)
            v = v_ref[0, rows, cols].astype(F32)
            kb = k * b_w
            items.append(dict(q=q, k=k, kb=kb, vb=v * b_w, kbe=kb * eg_w, qe=q * eg_w,
                              decay=decay, gcols=gcols))
    for it in items:
        k_bd = _blockdiag_rows(it["k"].astype(BF16), GDN_DK)
        aq = _nt_dot(jnp.concatenate([it["kb"].astype(BF16), it["q"].astype(BF16)], axis=0), k_bd)
        it["qk"] = aq[C:] * it["decay"]
        it["p"] = -jnp.where(strict, aq[:C] * it["decay"], 0.0)
        it["t"] = eye + it["p"]
    if n_sq > 0:
        for it in items:
            it["p"] = _packed_dot_split(it["p"], it["p"], C)
        for _ in range(n_sq - 1):
            for it in items:
                both = _packed_dot_split(jnp.concatenate([it["t"], it["p"]], axis=0), it["p"], C)
                it["t"] = it["t"] + both[:C]
                it["p"] = both[C:]
        for it in items:
            it["t"] = it["t"] + _packed_dot_split(it["t"], it["p"], C)
    local = []
    for it in items:
        rhs = jnp.concatenate([_blockdiag_rows(it["vb"].astype(BF16), GDN_DV),
                               _blockdiag_rows(it["kbe"].astype(BF16), GDN_DK)], axis=1)
        uw = _dot(it["t"].astype(BF16), rhs)
        local.append((uw[:, :W], uw[:, W:], it["qk"], it["qe"], it["k"], it["gcols"]))

    state = [s_ref[h] for h in range(GDN_HEADS)]
    for c in range(n_chunks):
        rows = slice(c * C, (c + 1) * C)
        groups = [local[c * n_groups + gi] for gi in range(n_groups)]
        head_slices = [slice(j * GDN_DK, (j + 1) * GDN_DK) for j in range(G)]
        r = [[_dot(jnp.concatenate([w[:, hs], qe[:, hs]], axis=0).astype(BF16),
                   state[gi * G + j].astype(BF16))
              for j, hs in enumerate(head_slices)]
             for gi, (u, w, qk, qe, k, gcols) in enumerate(groups)]
        v_new_bf = [[(u[:, hs] - r[gi][j][:C]).astype(BF16) for j, hs in enumerate(head_slices)]
                    for gi, (u, w, qk, qe, k, gcols) in enumerate(groups)]
        o = [jnp.concatenate([r[gi][j][C:] for j in range(G)], axis=1)
             + _dot(qk.astype(BF16), _blockdiag_rows(jnp.concatenate(v_new_bf[gi], axis=1), GDN_DV))
             for gi, (u, w, qk, qe, k, gcols) in enumerate(groups)]
        for gi, (u, w, qk, qe, k, gcols) in enumerate(groups):
            for j, hs in enumerate(head_slices):
                h = gi * G + j
                g_last = gcols[j][C - 1:C, :]
                k_dec = (k[:, hs] * jnp.exp(g_last - gcols[j])).astype(BF16)
                state[h] = state[h] * jnp.exp(g_last) + _tn_dot(k_dec, v_new_bf[gi][j])
        for gi in range(n_groups):
            for j, hs in enumerate(head_slices):
                h = gi * G + j
                oh = o[gi][:, hs]
                on = oh * lax.rsqrt(jnp.mean(oh * oh, axis=-1, keepdims=True) + RMS_EPS) * nw
                ocols = slice(h * GDN_DV, (h + 1) * GDN_DV)
                o_ref[0, rows, ocols] = (on * gate_ref[0, rows, ocols].astype(F32)).astype(BF16)

    for h in range(GDN_HEADS):
        s_ref[h] = state[h]
        sout_ref[0, h] = state[h]


def _gdn(qg, kg, vg, gate, gb, s0, norm_w, chunk, n_chunks):
    B, T, _ = qg.shape
    tb = chunk * n_chunks
    nt = T // tb
    tok_spec = lambda width: pl.BlockSpec((1, tb, width), lambda b, t: (b, t, 0))
    st_spec = pl.BlockSpec((1, GDN_HEADS, GDN_DK, GDN_DV), lambda b, t: (b, 0, 0, 0))
    return pl.pallas_call(
        functools.partial(_gdn_kernel, chunk=chunk, n_chunks=n_chunks),
        grid=(B, nt),
        in_specs=[tok_spec(GDN_QK), tok_spec(GDN_QK), tok_spec(GDN_V), tok_spec(GDN_V),
                  tok_spec(LANES), st_spec, pl.BlockSpec((1, GDN_DV), lambda b, t: (0, 0))],
        out_specs=(tok_spec(GDN_V), st_spec),
        out_shape=(jax.ShapeDtypeStruct((B, T, GDN_V), BF16),
                   jax.ShapeDtypeStruct((B, GDN_HEADS, GDN_DK, GDN_DV), F32)),
        scratch_shapes=[pltpu.VMEM((GDN_HEADS, GDN_DK, GDN_DV), F32)],
        compiler_params=pltpu.CompilerParams(
            dimension_semantics=("arbitrary", "arbitrary"),
            vmem_limit_bytes=VMEM_LIMIT_BYTES),
        name="gdn",
    )(qg, kg, vg, gate, gb, s0, norm_w)


def _lane_tile(x, width):
    if width % LANES == 0:
        return jnp.concatenate([x] * (width // LANES), axis=1)
    return x[:, :1]


def _attn_kernel(q_ref, k_ref, v_ref, lam_ref, w_ref, o_ref, m_ref, acc_ref, *, bq, bk, unroll,
                 q_pos0, lam_init):
    qi = pl.program_id(2)
    tk = k_ref.shape[1]
    n_blocks = tk // bk

    q = q_ref[0]
    lane = lax.broadcasted_iota(jnp.int32, q.shape, 1)
    zero = jnp.zeros_like(q)
    q_maps = [jnp.where(lane < DIFF_DH, q, zero), jnp.where(lane >= DIFF_DH, q, zero)]

    q_first = q_pos0 + qi * bq
    q_last = q_first + (bq - 1)
    vis_first = (q_first // CHUNK + 1) * CHUNK
    vis_last = (q_last // CHUNK + 1) * CHUNK
    n_full = jnp.minimum(vis_first // bk, n_blocks)
    n_all = jnp.minimum((vis_last + bk - 1) // bk, n_blocks)

    m_ref[...] = jnp.full(m_ref.shape, NEG_INF, F32)
    acc_ref[...] = jnp.zeros(acc_ref.shape, F32)
    ones = jnp.ones((bk, LANES), BF16)

    def steps(i, carry, masked, n, first):
        starts = [pl.multiple_of((first + i * n + j) * bk, bk) for j in range(n)]
        scores = [[_nt_dot(qm, k_ref[0, pl.ds(st, bk), :].astype(BF16)) for qm in q_maps]
                  for st in starts]
        m = [m_ref[mp] for mp in range(2)]
        acc = [acc_ref[mp] for mp in range(2)]
        for st, s_maps in zip(starts, scores):
            v_ext = jnp.concatenate([v_ref[0, pl.ds(st, bk), :].astype(BF16), ones], axis=1)
            for mp, s in enumerate(s_maps):
                if masked:
                    kpos = st + lax.broadcasted_iota(jnp.int32, (bq, bk), 1)
                    qpos = q_first + lax.broadcasted_iota(jnp.int32, (bq, bk), 0)
                    s = jnp.where(kpos // CHUNK <= qpos // CHUNK, s, NEG_INF)
                m_new = jnp.maximum(m[mp], jnp.max(s, axis=-1, keepdims=True))
                p = jnp.exp(s - _lane_tile(m_new, bk))
                alpha = jnp.exp(m[mp] - m_new)
                acc[mp] = (jnp.concatenate([alpha, alpha], axis=1) * acc[mp]
                           + _dot(p.astype(BF16), v_ext))
                m[mp] = m_new
        for mp in range(2):
            m_ref[mp] = m[mp]
            acc_ref[mp] = acc[mp]
        return carry

    done = 0
    u = unroll
    while u >= 1:
        trips = (n_full - done) // u
        lax.fori_loop(0, trips, functools.partial(steps, masked=False, n=u, first=done), 0)
        done = done + trips * u
        u //= 2
    lax.fori_loop(0, n_all - n_full, functools.partial(steps, masked=True, n=1, first=n_full), 0)

    lv = lam_ref[...]
    lam = (jnp.exp(jnp.sum(lv[0:1] * lv[1:2], axis=-1, keepdims=True))
           - jnp.exp(jnp.sum(lv[2:3] * lv[3:4], axis=-1, keepdims=True)) + lam_init)
    o = [acc_ref[mp, :, :DIFF_DV] / acc_ref[mp, :, DIFF_DV:] for mp in range(2)]
    out = o[0] - lam * o[1]
    out = out * lax.rsqrt(jnp.mean(out * out, axis=-1, keepdims=True) + RMS_EPS) * w_ref[...]
    o_ref[0] = (out * (1.0 - lam_init)).astype(BF16)


def _attn(qd, k_all, v_all, lam_p, subln_w, bq, bk, unroll, q_pos0, lam_init):
    B, Tq, _ = qd.shape
    Tk = k_all.shape[1]
    nq = Tq // bq
    return pl.pallas_call(
        functools.partial(_attn_kernel, bq=bq, bk=bk, unroll=unroll, q_pos0=q_pos0,
                          lam_init=lam_init),
        grid=(B, DIFF_HEADS, nq),
        in_specs=[pl.BlockSpec((1, bq, LANES), lambda b, h, i: (b, i, h)),
                  pl.BlockSpec((1, Tk, LANES), lambda b, h, i: (b, 0, h)),
                  pl.BlockSpec((1, Tk, LANES), lambda b, h, i: (b, 0, h)),
                  pl.BlockSpec(lam_p.shape, lambda b, h, i: (0, 0)),
                  pl.BlockSpec(subln_w.shape, lambda b, h, i: (0, 0))],
        out_specs=pl.BlockSpec((1, bq, LANES), lambda b, h, i: (b, i, h)),
        out_shape=jax.ShapeDtypeStruct((B, Tq, DIFF_V), BF16),
        scratch_shapes=[pltpu.VMEM((2, bq, LANES), F32),
                        pltpu.VMEM((2, bq, DIFF_DV + LANES), F32)],
        compiler_params=pltpu.CompilerParams(
            dimension_semantics=("arbitrary", "arbitrary", "arbitrary"),
            vmem_limit_bytes=VMEM_LIMIT_BYTES),
        name="diffattn",
    )(qd, k_all, v_all, lam_p, subln_w)


def _ffn_kernel(x_ref, oa_ref, ob_ref, wo_ref, g1_ref, b1_ref, wup_ref, cw_ref, cb_ref, cbuf_ref,
                wdn_ref, g2_ref, b2_ref, y_ref, cstate_ref, prev_ref, *, alpha, d_ff):
    t = pl.program_id(1)

    @pl.when(t == 0)
    def _():
        prev_ref[...] = cbuf_ref[0]

    tm = x_ref.shape[1]
    mix = _dot(oa_ref[0], wo_ref[:GDN_V, :]) + _dot(ob_ref[0], wo_ref[GDN_V:, :])
    x1 = _layer_norm(alpha * x_ref[0] + mix, g1_ref[...], b1_ref[...])
    up = _dot(x1.astype(BF16), wup_ref[...])
    prev = prev_ref[...]
    cw = cw_ref[...]
    u = up * cw[FFN_CONV - 1:FFN_CONV] + cb_ref[...]
    for j in range(FFN_CONV - 1):
        u = u + _shifted_rows(prev, up, FFN_CONV - 1 - j) * cw[j:j + 1]
    tail = up[tm - SUBLANES:]
    prev_ref[...] = tail
    cstate_ref[0] = tail
    hh = _silu(u[:, :d_ff]) * u[:, d_ff:]
    y = _dot(hh.astype(BF16), wdn_ref[...])
    y_ref[0] = _layer_norm(alpha * x1 + y, g2_ref[...], b2_ref[...])


def _ffn(x, oa, ob, w_o, g1, b1, w_up, cw, cb, cbuf8, w_dn, g2, b2, tm, alpha):
    B, T, D = x.shape
    nt = T // tm
    d_ff = w_dn.shape[0]
    tok_spec = lambda width: pl.BlockSpec((1, tm, width), lambda b, t: (b, t, 0))
    full = lambda a: pl.BlockSpec(a.shape, lambda b, t: (0,) * a.ndim)
    st_spec = pl.BlockSpec((1, SUBLANES, 2 * d_ff), lambda b, t: (b, 0, 0))
    return pl.pallas_call(
        functools.partial(_ffn_kernel, alpha=alpha, d_ff=d_ff),
        grid=(B, nt),
        in_specs=[tok_spec(D), tok_spec(GDN_V), tok_spec(DIFF_V), full(w_o), full(g1), full(b1),
                  full(w_up), full(cw), full(cb), st_spec, full(w_dn), full(g2), full(b2)],
        out_specs=(tok_spec(D), st_spec),
        out_shape=(jax.ShapeDtypeStruct((B, T, D), F32),
                   jax.ShapeDtypeStruct((B, SUBLANES, 2 * d_ff), F32)),
        scratch_shapes=[pltpu.VMEM((SUBLANES, 2 * d_ff), F32)],
        compiler_params=pltpu.CompilerParams(
            dimension_semantics=("arbitrary", "arbitrary"),
            vmem_limit_bytes=VMEM_LIMIT_BYTES),
        name="ffn",
    )(x, oa, ob, w_o, g1, b1, w_up, cw, cb, cbuf8, w_dn, g2, b2)


def _pad_rows_front(buf, rows):
    B, r, C = buf.shape
    return jnp.concatenate([jnp.zeros((B, rows - r, C), buf.dtype), buf], axis=1)


def _lane_row(vec):
    n = vec.shape[0]
    pad = (-n) % LANES
    return jnp.pad(vec.astype(F32), (0, pad)).reshape(1, n + pad)


def _rope_tables(pos0, T):
    half = DIFF_DH // 2
    inv = ROPE_THETA ** (-jnp.arange(half, dtype=F32) * (2.0 / DIFF_DH))
    pos = pos0 + jnp.arange(T, dtype=jnp.int32)
    ang = pos.astype(F32)[:, None] * inv[None, :]
    cos = jnp.cos(ang)
    sin = jnp.sin(ang)
    reps = LANES // DIFF_DH
    cos_t = jnp.tile(jnp.concatenate([cos, cos], axis=-1), (1, reps))
    sin_t = jnp.tile(jnp.concatenate([-sin, sin], axis=-1), (1, reps))
    return cos_t, sin_t


def _largest_divisor(n, cap, mult):
    best = None
    for d in range(mult, min(n, cap) + 1, mult):
        if n % d == 0:
            best = d
    return best if best is not None else n


def _layer(x, pos0, k_past, v_past, s0, conv_qkv_buf, conv_ffn_buf, p, lam_init, alpha):
    B, T, _ = x.shape
    tm = _largest_divisor(T, 512, SUBLANES)
    cos_t, sin_t = _rope_tables(pos0, T)
    (qg, kg, vg, gate, gb, qd, kd, vd, k_new, v_new, cq8) = _inproj(
        x, p["w_in"], p["gdn_conv_w"], _pad_rows_front(conv_qkv_buf, SUBLANES),
        p["alog"], p["dtb"], cos_t, sin_t, tm)

    chunk = CHUNK if T % CHUNK == 0 else T
    n_chunks = _largest_divisor(T // chunk, 4, 1)
    oa, s_new = _gdn(qg, kg, vg, gate, gb, s0, p["gdn_norm_w"], chunk, n_chunks)

    if k_past is None:
        k_all, v_all = kd, vd
    else:
        k_all = jnp.concatenate([k_past.reshape(B, -1, DIFF_QK).astype(BF16), kd], axis=1)
        v_all = jnp.concatenate([v_past.reshape(B, -1, DIFF_V).astype(BF16), vd], axis=1)
    Tk = k_all.shape[1]
    bq = _largest_divisor(T, 512, 16)
    bk = _largest_divisor(Tk, 1536 if Tk % 512 else 512, 16)
    ob = _attn(qd, k_all, v_all, p["diff_lambda"], p["diff_subln_w"], bq, bk, 4, pos0, lam_init)

    tm_f = _largest_divisor(T, 256, SUBLANES)
    y, cf8 = _ffn(x, oa, ob, p["w_o"], p["ln1_g"], p["ln1_b"], p["w_up"], p["ffn_conv_w"],
                  p["ffn_conv_b"], _pad_rows_front(conv_ffn_buf, SUBLANES), p["w_down"],
                  p["ln2_g"], p["ln2_b"], tm_f, alpha)

    new_k = k_new.reshape(B, T, DIFF_HEADS, 2 * DIFF_DH)
    new_v = v_new.reshape(B, T, DIFF_HEADS, DIFF_DV)
    return (y, new_k, new_v, s_new, cq8[:, SUBLANES - (GDN_CONV - 1):],
            cf8[:, SUBLANES - (FFN_CONV - 1):])


def _prep_params(l, w_in, gdn_conv_w, gdn_a_log, gdn_dt_bias, gdn_norm_w, diff_lambda,
                 diff_subln_w, w_o, ln1_g, ln1_b, w_up, ffn_conv_w, ffn_conv_b, w_down,
                 ln2_g, ln2_b):
    o1 = GDN_QKV
    o2 = o1 + GDN_V
    o3 = o2 + GDN_HEADS
    o4 = o3 + GDN_HEADS
    w = w_in[l]
    d_model = w.shape[0]
    w_r = jnp.concatenate(
        [w[:, :o2], w[:, o4:], w[:, o2:o4],
         jnp.zeros((d_model, LANES - 2 * GDN_HEADS), w.dtype)], axis=1).astype(BF16)
    row = lambda v: v.astype(F32).reshape(1, -1)
    return {
        "w_in": w_r, "gdn_conv_w": gdn_conv_w[l].astype(F32),
        "alog": _lane_row(gdn_a_log[l]), "dtb": _lane_row(gdn_dt_bias[l]),
        "gdn_norm_w": row(gdn_norm_w[l]), "diff_lambda": diff_lambda[l].astype(F32),
        "diff_subln_w": row(diff_subln_w[l]), "w_o": w_o[l].astype(BF16),
        "ln1_g": row(ln1_g[l]), "ln1_b": row(ln1_b[l]), "w_up": w_up[l].astype(BF16),
        "ffn_conv_w": ffn_conv_w[l].astype(F32), "ffn_conv_b": row(ffn_conv_b[l]),
        "w_down": w_down[l].astype(BF16), "ln2_g": row(ln2_g[l]), "ln2_b": row(ln2_b[l]),
    }


def kernel(x_prompt, x_sample, cache_k, cache_v, state_gdn, state_conv_qkv, state_conv_ffn, w_in, gdn_conv_w, gdn_a_log, gdn_dt_bias, gdn_norm_w, diff_lambda, diff_subln_w, w_o, ln1_g, ln1_b, w_up, ffn_conv_w, ffn_conv_b, w_down, ln2_g, ln2_b):
    depth = w_in.shape[0]
    Bp = x_prompt.shape[0]
    past = cache_k.shape[2]
    alpha = (2 * depth) ** 0.25
    xp, xs = x_prompt, x_sample
    outs_p, outs_s = [], []
    for l in range(depth):
        p = _prep_params(l, w_in, gdn_conv_w, gdn_a_log, gdn_dt_bias, gdn_norm_w, diff_lambda,
                         diff_subln_w, w_o, ln1_g, ln1_b, w_up, ffn_conv_w, ffn_conv_b, w_down,
                         ln2_g, ln2_b)
        lam_init = 0.8 - 0.6 * math.exp(-0.3 * l)
        s0_p = jnp.zeros((Bp, GDN_HEADS, GDN_DK, GDN_DV), F32)
        cq0_p = jnp.zeros((Bp, GDN_CONV - 1, GDN_QKV), xp.dtype)
        cf0_p = jnp.zeros((Bp, FFN_CONV - 1, state_conv_ffn.shape[-1]), xp.dtype)
        xp, kp, vp, sp, cqp, cfp = _layer(xp, 0, None, None, s0_p, cq0_p, cf0_p, p, lam_init, alpha)
        xs, ksn, vsn, ssn, cqs, cfs = _layer(xs, past, cache_k[l], cache_v[l], state_gdn[l],
                                             state_conv_qkv[l], state_conv_ffn[l], p, lam_init,
                                             alpha)
        outs_p.append((kp, vp, sp, cqp, cfp))
        outs_s.append((ksn, vsn, ssn, cqs, cfs))

    def stk(outs, i):
        return jnp.stack([o[i] for o in outs])

    return (xp, xs,
            stk(outs_p, 0), stk(outs_p, 1), stk(outs_p, 2), stk(outs_p, 3), stk(outs_p, 4),
            stk(outs_s, 0), stk(outs_s, 1), stk(outs_s, 2), stk(outs_s, 3), stk(outs_s, 4))
```

```python
import functools
import math

import jax
import jax.numpy as jnp
from jax import lax
from jax.experimental import pallas as pl
from jax.experimental.pallas import tpu as pltpu

F32 = jnp.float32
BF16 = jnp.bfloat16

LANES = 128
SUBLANES = 8
VMEM_LIMIT_BYTES = 56 * 1024 * 1024
FFN_CHUNK_COLS = 2816

CHUNK = 64
GDN_HEADS = 4
GDN_DK = 128
GDN_DV = 128
GDN_CONV = 4
DIFF_HEADS = 4
DIFF_DH = 64
DIFF_DV = 2 * DIFF_DH
ROPE_THETA = 10000.0
FFN_CONV = 3
LN_EPS = 1e-5
RMS_EPS = 1e-6
L2_EPS = 1e-6
NEG_INF = -1e30

GDN_QK = GDN_HEADS * GDN_DK
GDN_V = GDN_HEADS * GDN_DV
GDN_QKV = 2 * GDN_QK + GDN_V
DIFF_QK = DIFF_HEADS * 2 * DIFF_DH
DIFF_V = DIFF_HEADS * DIFF_DV

COL_GATE = GDN_QKV
COL_QB = COL_GATE + GDN_V
COL_KB = COL_QB + DIFF_QK
COL_VB = COL_KB + DIFF_QK
COL_AB = COL_VB + DIFF_V
D_IN_PAD = COL_AB + LANES


def _nt_dot(a, b):
    return lax.dot_general(a, b, (((1,), (1,)), ((), ())), preferred_element_type=F32)


def _tn_dot(a, b):
    return lax.dot_general(a, b, (((0,), (0,)), ((), ())), preferred_element_type=F32)


def _dot(a, b):
    return jnp.dot(a, b, preferred_element_type=F32)


def _dot_hi(a, b):
    return jnp.dot(a, b, preferred_element_type=F32, precision=lax.Precision.HIGHEST)


def _sigmoid(x):
    return 1.0 / (1.0 + jnp.exp(-x))


def _silu(x):
    return x * _sigmoid(x)


def _softplus(x):
    return jnp.maximum(x, 0.0) + jnp.log(1.0 + jnp.exp(-jnp.abs(x)))


def _layer_norm(x, g, b):
    mu = jnp.mean(x, axis=-1, keepdims=True)
    xc = x - mu
    var = jnp.mean(xc * xc, axis=-1, keepdims=True)
    return xc * lax.rsqrt(var + LN_EPS) * g + b


def _shifted_rows(prev, cur, back):
    if back == 0:
        return cur
    ext = jnp.concatenate([prev, cur], axis=0)
    n = cur.shape[0]
    return ext[SUBLANES - back:SUBLANES - back + n]


def _inproj_kernel(x_ref, w_ref, convw_ref, cbuf_ref, alog_ref, dtb_ref, cos_ref, sin_ref,
                   qg_ref, kg_ref, vg_ref, gate_ref, gb_ref, qd_ref, kd_ref, vd_ref,
                   kout_ref, vout_ref, cstate_ref, prev_ref):
    t = pl.program_id(1)

    @pl.when(t == 0)
    def _():
        prev_ref[...] = cbuf_ref[0]

    tm = x_ref.shape[1]
    x = x_ref[0].astype(BF16)
    proj = lambda lo, hi: _dot(x, w_ref[:, lo:hi])

    def gdn_qkv(qkv):
        prev = prev_ref[...]
        cw = convw_ref[...]
        y = qkv * cw[GDN_CONV - 1:GDN_CONV]
        for j in range(GDN_CONV - 1):
            y = y + _shifted_rows(prev, qkv, GDN_CONV - 1 - j) * cw[j:j + 1]
        tail = qkv[tm - SUBLANES:]
        prev_ref[...] = tail
        cstate_ref[0] = tail
        y = _silu(y)
        for hd in range(GDN_HEADS):
            for base, ref in ((0, qg_ref), (GDN_QK, kg_ref)):
                z = y[:, base + hd * GDN_DK: base + (hd + 1) * GDN_DK]
                z = z * lax.rsqrt(jnp.sum(z * z, axis=-1, keepdims=True) + L2_EPS)
                ref[0, :, hd * GDN_DK:(hd + 1) * GDN_DK] = z.astype(BF16)
        vg_ref[0] = y[:, 2 * GDN_QK:].astype(BF16)

    def rope(z):
        first_half = (lax.broadcasted_iota(jnp.int32, z.shape, 1) % DIFF_DH) < (DIFF_DH // 2)
        swapped = jnp.where(first_half, pltpu.roll(z, LANES - DIFF_DH // 2, 1),
                            pltpu.roll(z, DIFF_DH // 2, 1))
        return z * cos_ref[...] + swapped * sin_ref[...]

    head_rows = lambda hd: pl.ds(hd, tm, stride=DIFF_HEADS)

    def gate_and_q(hm):
        gate_ref[0] = _silu(hm[:, :GDN_V]).astype(BF16)
        scale = DIFF_DH ** -0.5 * math.log2(math.e)
        for hd in range(DIFF_HEADS):
            sl = slice(hd * LANES, (hd + 1) * LANES)
            qd_ref[0, :, sl] = (rope(hm[:, GDN_V + hd * LANES:GDN_V + (hd + 1) * LANES])
                                * scale).astype(BF16)

    def k_v_gates(hl):
        for hd in range(DIFF_HEADS):
            sl = slice(hd * LANES, (hd + 1) * LANES)
            r = rope(hl[:, sl])
            kout_ref[0, head_rows(hd), :] = r
            kd_ref[0, :, sl] = r.astype(BF16)
        vb = hl[:, DIFF_QK:DIFF_QK + DIFF_V]
        for hd in range(DIFF_HEADS):
            vout_ref[0, head_rows(hd), :] = vb[:, hd * DIFF_DV:(hd + 1) * DIFF_DV]
        vd_ref[0] = vb.astype(BF16)
        ab = hl[:, DIFF_QK + DIFF_V:]
        g = -jnp.exp(alog_ref[...]) * _softplus(ab + dtb_ref[...])
        lane = lax.broadcasted_iota(jnp.int32, ab.shape, 1)
        gb_ref[0] = jnp.where(lane < GDN_HEADS, g, _sigmoid(ab))

    h_qkv = proj(0, COL_GATE)
    h_mid = proj(COL_GATE, COL_KB)
    gdn_qkv(h_qkv)
    h_last = proj(COL_KB, D_IN_PAD)
    gate_and_q(h_mid)
    k_v_gates(h_last)


def _inproj(x, w_in_r, conv_w, cbuf8, alog, dtb, cos_t, sin_t, tm):
    B, T, D = x.shape
    nt = T // tm
    tok = lambda width, dt: jax.ShapeDtypeStruct((B, T, width), dt)
    tok_spec = lambda width: pl.BlockSpec((1, tm, width), lambda b, t: (b, t, 0))
    full = lambda a: pl.BlockSpec(a.shape, lambda b, t: (0,) * a.ndim)
    out_shape = (tok(GDN_QK, BF16), tok(GDN_QK, BF16), tok(GDN_V, BF16), tok(GDN_V, BF16),
                 tok(LANES, F32), tok(DIFF_QK, BF16), tok(DIFF_QK, BF16), tok(DIFF_V, BF16),
                 jax.ShapeDtypeStruct((B, T * DIFF_HEADS, 2 * DIFF_DH), F32),
                 jax.ShapeDtypeStruct((B, T * DIFF_HEADS, DIFF_DV), F32),
                 jax.ShapeDtypeStruct((B, SUBLANES, GDN_QKV), F32))
    out_specs = (tok_spec(GDN_QK), tok_spec(GDN_QK), tok_spec(GDN_V), tok_spec(GDN_V),
                 tok_spec(LANES), tok_spec(DIFF_QK), tok_spec(DIFF_QK), tok_spec(DIFF_V),
                 pl.BlockSpec((1, tm * DIFF_HEADS, 2 * DIFF_DH), lambda b, t: (b, t, 0)),
                 pl.BlockSpec((1, tm * DIFF_HEADS, DIFF_DV), lambda b, t: (b, t, 0)),
                 pl.BlockSpec((1, SUBLANES, GDN_QKV), lambda b, t: (b, 0, 0)))
    in_specs = [tok_spec(D), full(w_in_r), full(conv_w),
                pl.BlockSpec((1, SUBLANES, GDN_QKV), lambda b, t: (b, 0, 0)),
                full(alog), full(dtb),
                pl.BlockSpec((tm, LANES), lambda b, t: (t, 0)),
                pl.BlockSpec((tm, LANES), lambda b, t: (t, 0))]
    return pl.pallas_call(
        _inproj_kernel,
        grid=(B, nt),
        in_specs=in_specs,
        out_specs=out_specs,
        out_shape=out_shape,
        scratch_shapes=[pltpu.VMEM((SUBLANES, GDN_QKV), F32)],
        compiler_params=pltpu.CompilerParams(
            dimension_semantics=("arbitrary", "arbitrary"),
            vmem_limit_bytes=VMEM_LIMIT_BYTES),
        name="inproj",
    )(x, w_in_r, conv_w, cbuf8, alog, dtb, cos_t, sin_t)


def _split_hi_lo(x):
    hi = x.astype(BF16)
    return hi, (x - hi.astype(F32)).astype(BF16)


def _blockdiag_rows(x, width):
    n = x.shape[1] // width
    blk = lax.broadcasted_iota(jnp.int32, x.shape, 1) // width
    zero = jnp.zeros_like(x)
    return jnp.concatenate([jnp.where(blk == j, x, zero) for j in range(n)], axis=0)


def _packed_dot_split(l2, r2, width):
    l_hi, l_lo = _split_hi_lo(l2)
    r_hi, r_lo = _split_hi_lo(r2)
    bd_hi = _blockdiag_rows(r_hi, width)
    bd_lo = _blockdiag_rows(r_lo, width)
    lhs = jnp.concatenate([l_hi, l_lo, l_hi, l_lo], axis=1)
    rhs = jnp.concatenate([bd_hi, bd_hi, bd_lo, bd_lo], axis=0)
    return _dot(lhs, rhs)


def _per_head(cols, lane_head):
    out = cols[0]
    for j in range(1, len(cols)):
        out = jnp.where(lane_head >= j, cols[j], out)
    return out


def _gdn_kernel(q_ref, k_ref, v_ref, gate_ref, gb_ref, s0_ref, nw_ref,
                o_ref, sout_ref, s_ref, *, chunk, n_chunks):
    t = pl.program_id(1)

    @pl.when(t == 0)
    def _():
        s_ref[...] = s0_ref[0]

    C = chunk
    G = LANES // C
    W = G * GDN_DK
    n_groups = GDN_HEADS // G
    row = lax.broadcasted_iota(jnp.int32, (C, LANES), 0)
    lane = lax.broadcasted_iota(jnp.int32, (C, LANES), 1)
    lane_in = lane % C
    lane_head_p = lane // C
    lane_head_w = lax.broadcasted_iota(jnp.int32, (C, W), 1) // GDN_DK
    incl = row >= lane_in
    strict = row > lane_in
    eye_b = row == lane_in
    eye = eye_b.astype(F32)
    r1 = lax.broadcasted_iota(jnp.int32, (C, C), 0)
    c1 = lax.broadcasted_iota(jnp.int32, (C, C), 1)
    tri = (r1 >= c1).astype(F32)
    scale = GDN_DK ** -0.5
    nw = nw_ref[...]
    n_sq = max(int(math.ceil(math.log2(C))) - 1, 0)

    items = []
    for c in range(n_chunks):
        rows = slice(c * C, (c + 1) * C)
        gbc = gb_ref[0, rows, :]
        gc_all = _dot_hi(tri, gbc)
        for gi in range(n_groups):
            heads = [gi * G + j for j in range(G)]
            cols = slice(gi * W, (gi + 1) * W)
            gcols = [gc_all[:, h:h + 1] for h in heads]
            bcols = [gbc[:, GDN_HEADS + h:GDN_HEADS + h + 1] for h in heads]
            gcol_p = _per_head(gcols, lane_head_p)
            grow_p = jnp.sum(jnp.where(eye_b, gcol_p, 0.0), axis=0, keepdims=True)
            decay = jnp.exp(jnp.where(incl, gcol_p - grow_p, -jnp.inf))
            b_w = _per_head(bcols, lane_head_w)
            eg_w = _per_head([jnp.exp(g) for g in gcols], lane_head_w)
            q = q_ref[0, rows, cols].astype(F32) * scale
            k = k_ref[0, rows, cols].astype(F32)
            v = v_ref[0, rows, cols].astype(F32)
            kb = k * b_w
            items.append(dict(q=q, k=k, kb=kb, vb=v * b_w, kbe=kb * eg_w, qe=q * eg_w,
                              decay=decay, gcols=gcols))
    for it in items:
        k_bd = _blockdiag_rows(it["k"].astype(BF16), GDN_DK)
        aq = _nt_dot(jnp.concatenate([it["kb"].astype(BF16), it["q"].astype(BF16)], axis=0), k_bd)
        it["qk"] = aq[C:] * it["decay"]
        it["p"] = -jnp.where(strict, aq[:C] * it["decay"], 0.0)
        it["t"] = eye + it["p"]
    if n_sq > 0:
        for it in items:
            it["p"] = _packed_dot_split(it["p"], it["p"], C)
        for _ in range(n_sq - 1):
            for it in items:
                both = _packed_dot_split(jnp.concatenate([it["t"], it["p"]], axis=0), it["p"], C)
                it["t"] = it["t"] + both[:C]
                it["p"] = both[C:]
        for it in items:
            it["t"] = it["t"] + _packed_dot_split(it["t"], it["p"], C)
    local = []
    for it in items:
        rhs = jnp.concatenate([_blockdiag_rows(it["vb"].astype(BF16), GDN_DV),
                               _blockdiag_rows(it["kbe"].astype(BF16), GDN_DK)], axis=1)
        uw = _dot(it["t"].astype(BF16), rhs)
        local.append((uw[:, :W], uw[:, W:], it["qk"], it["qe"], it["k"], it["gcols"]))

    state = [s_ref[h] for h in range(GDN_HEADS)]
    for c in range(n_chunks):
        rows = slice(c * C, (c + 1) * C)
        groups = [local[c * n_groups + gi] for gi in range(n_groups)]
        head_slices = [slice(j * GDN_DK, (j + 1) * GDN_DK) for j in range(G)]
        r = [[_dot(jnp.concatenate([w[:, hs], qe[:, hs]], axis=0).astype(BF16),
                   state[gi * G + j].astype(BF16))
              for j, hs in enumerate(head_slices)]
             for gi, (u, w, qk, qe, k, gcols) in enumerate(groups)]
        v_new_bf = [[(u[:, hs] - r[gi][j][:C]).astype(BF16) for j, hs in enumerate(head_slices)]
                    for gi, (u, w, qk, qe, k, gcols) in enumerate(groups)]
        o = [jnp.concatenate([r[gi][j][C:] for j in range(G)], axis=1)
             + _dot(qk.astype(BF16), _blockdiag_rows(jnp.concatenate(v_new_bf[gi], axis=1), GDN_DV))
             for gi, (u, w, qk, qe, k, gcols) in enumerate(groups)]
        for gi, (u, w, qk, qe, k, gcols) in enumerate(groups):
            for j, hs in enumerate(head_slices):
                h = gi * G + j
                g_last = gcols[j][C - 1:C, :]
                k_dec = (k[:, hs] * jnp.exp(g_last - gcols[j])).astype(BF16)
                state[h] = state[h] * jnp.exp(g_last) + _tn_dot(k_dec, v_new_bf[gi][j])
        for gi in range(n_groups):
            for j, hs in enumerate(head_slices):
                h = gi * G + j
                oh = o[gi][:, hs]
                on = oh * lax.rsqrt(jnp.mean(oh * oh, axis=-1, keepdims=True) + RMS_EPS) * nw
                ocols = slice(h * GDN_DV, (h + 1) * GDN_DV)
                o_ref[0, rows, ocols] = (on * gate_ref[0, rows, ocols].astype(F32)).astype(BF16)

    for h in range(GDN_HEADS):
        s_ref[h] = state[h]
        sout_ref[0, h] = state[h]


def _gdn(qg, kg, vg, gate, gb, s0, norm_w, chunk, n_chunks):
    B, T, _ = qg.shape
    tb = chunk * n_chunks
    nt = T // tb
    tok_spec = lambda width: pl.BlockSpec((1, tb, width), lambda b, t: (b, t, 0))
    st_spec = pl.BlockSpec((1, GDN_HEADS, GDN_DK, GDN_DV), lambda b, t: (b, 0, 0, 0))
    return pl.pallas_call(
        functools.partial(_gdn_kernel, chunk=chunk, n_chunks=n_chunks),
        grid=(B, nt),
        in_specs=[tok_spec(GDN_QK), tok_spec(GDN_QK), tok_spec(GDN_V), tok_spec(GDN_V),
                  tok_spec(LANES), st_spec, pl.BlockSpec((1, GDN_DV), lambda b, t: (0, 0))],
        out_specs=(tok_spec(GDN_V), st_spec),
        out_shape=(jax.ShapeDtypeStruct((B, T, GDN_V), BF16),
                   jax.ShapeDtypeStruct((B, GDN_HEADS, GDN_DK, GDN_DV), F32)),
        scratch_shapes=[pltpu.VMEM((GDN_HEADS, GDN_DK, GDN_DV), F32)],
        compiler_params=pltpu.CompilerParams(
            dimension_semantics=("arbitrary", "arbitrary"),
            vmem_limit_bytes=VMEM_LIMIT_BYTES),
        name="gdn",
    )(qg, kg, vg, gate, gb, s0, norm_w)


def _lane_tile(x, width):
    if width % LANES == 0:
        return jnp.concatenate([x] * (width // LANES), axis=1)
    return x[:, :1]


def _attn_kernel(q_ref, k_ref, v_ref, lam_ref, w_ref, o_ref, m_ref, acc_ref, *, bq, bk, unroll,
                 q_pos0, lam_init):
    qi = pl.program_id(2)
    tk = k_ref.shape[1]
    n_blocks = tk // bk

    q = q_ref[0]
    lane = lax.broadcasted_iota(jnp.int32, q.shape, 1)
    zero = jnp.zeros_like(q)
    q_maps = [jnp.where(lane < DIFF_DH, q, zero), jnp.where(lane >= DIFF_DH, q, zero)]

    q_first = q_pos0 + qi * bq
    q_last = q_first + (bq - 1)
    vis_first = (q_first // CHUNK + 1) * CHUNK
    vis_last = (q_last // CHUNK + 1) * CHUNK
    n_full = jnp.minimum(vis_first // bk, n_blocks)
    n_all = jnp.minimum((vis_last + bk - 1) // bk, n_blocks)

    m_ref[...] = jnp.full(m_ref.shape, NEG_INF, F32)
    acc_ref[...] = jnp.zeros(acc_ref.shape, F32)
    ones = jnp.ones((bk, LANES), BF16)

    def steps(i, carry, masked, n, first):
        starts = [pl.multiple_of((first + i * n + j) * bk, bk) for j in range(n)]
        scores = [[_nt_dot(qm, k_ref[0, pl.ds(st, bk), :].astype(BF16)) for qm in q_maps]
                  for st in starts]
        m = [m_ref[mp] for mp in range(2)]
        acc = [acc_ref[mp] for mp in range(2)]
        for st, s_maps in zip(starts, scores):
            v_ext = jnp.concatenate([v_ref[0, pl.ds(st, bk), :].astype(BF16), ones], axis=1)
            for mp, s in enumerate(s_maps):
                if masked:
                    kpos = st + lax.broadcasted_iota(jnp.int32, (bq, bk), 1)
                    qpos = q_first + lax.broadcasted_iota(jnp.int32, (bq, bk), 0)
                    s = jnp.where(kpos // CHUNK <= qpos // CHUNK, s, NEG_INF)
                m_new = jnp.maximum(m[mp], jnp.max(s, axis=-1, keepdims=True))
                p = jnp.exp2(s - _lane_tile(m_new, bk))
                alpha = jnp.exp2(m[mp] - m_new)
                acc[mp] = (jnp.concatenate([alpha, alpha], axis=1) * acc[mp]
                           + _dot(p.astype(BF16), v_ext))
                m[mp] = m_new
        for mp in range(2):
            m_ref[mp] = m[mp]
            acc_ref[mp] = acc[mp]
        return carry

    done = 0
    u = unroll
    while u >= 1:
        trips = (n_full - done) // u
        lax.fori_loop(0, trips, functools.partial(steps, masked=False, n=u, first=done), 0)
        done = done + trips * u
        u //= 2
    lax.fori_loop(0, n_all - n_full, functools.partial(steps, masked=True, n=1, first=n_full), 0)

    lv = lam_ref[...]
    lam = (jnp.exp(jnp.sum(lv[0:1] * lv[1:2], axis=-1, keepdims=True))
           - jnp.exp(jnp.sum(lv[2:3] * lv[3:4], axis=-1, keepdims=True)) + lam_init)
    o = [acc_ref[mp, :, :DIFF_DV] / acc_ref[mp, :, DIFF_DV:] for mp in range(2)]
    out = o[0] - lam * o[1]
    out = out * lax.rsqrt(jnp.mean(out * out, axis=-1, keepdims=True) + RMS_EPS) * w_ref[...]
    o_ref[0] = (out * (1.0 - lam_init)).astype(BF16)


def _attn(qd, k_all, v_all, lam_p, subln_w, bq, bk, unroll, q_pos0, lam_init):
    B, Tq, _ = qd.shape
    Tk = k_all.shape[1]
    nq = Tq // bq
    return pl.pallas_call(
        functools.partial(_attn_kernel, bq=bq, bk=bk, unroll=unroll, q_pos0=q_pos0,
                          lam_init=lam_init),
        grid=(B, DIFF_HEADS, nq),
        in_specs=[pl.BlockSpec((1, bq, LANES), lambda b, h, i: (b, i, h)),
                  pl.BlockSpec((1, Tk, LANES), lambda b, h, i: (b, 0, h)),
                  pl.BlockSpec((1, Tk, LANES), lambda b, h, i: (b, 0, h)),
                  pl.BlockSpec(lam_p.shape, lambda b, h, i: (0, 0)),
                  pl.BlockSpec(subln_w.shape, lambda b, h, i: (0, 0))],
        out_specs=pl.BlockSpec((1, bq, LANES), lambda b, h, i: (b, i, h)),
        out_shape=jax.ShapeDtypeStruct((B, Tq, DIFF_V), BF16),
        scratch_shapes=[pltpu.VMEM((2, bq, LANES), F32),
                        pltpu.VMEM((2, bq, DIFF_DV + LANES), F32)],
        compiler_params=pltpu.CompilerParams(
            dimension_semantics=("arbitrary", "arbitrary", "arbitrary"),
            vmem_limit_bytes=VMEM_LIMIT_BYTES),
        name="diffattn",
    )(qd, k_all, v_all, lam_p, subln_w)


def _ffn_kernel(x_ref, oa_ref, ob_ref, wo_ref, g1_ref, b1_ref, wup_ref, cw_ref, cb_ref, cbuf_ref,
                wdn_ref, g2_ref, b2_ref, y_ref, cstate_ref, prev_ref, *, alpha, d_ff,
                chunk_cols):
    t = pl.program_id(1)

    @pl.when(t == 0)
    def _():
        prev_ref[...] = cbuf_ref[0]

    tm = x_ref.shape[1]
    mix = _dot(oa_ref[0], wo_ref[:GDN_V, :]) + _dot(ob_ref[0], wo_ref[GDN_V:, :])
    x1 = _layer_norm(alpha * x_ref[0] + mix, g1_ref[...], b1_ref[...])
    x1b = x1.astype(BF16)

    def conv_up(col, width):
        cols = slice(col, col + width)
        up = _dot(x1b, wup_ref[:, cols])
        prev = prev_ref[:, cols]
        cw = cw_ref[:, cols]
        u = up * cw[FFN_CONV - 1:FFN_CONV] + cb_ref[:, cols]
        for j in range(FFN_CONV - 1):
            u = u + _shifted_rows(prev, up, FFN_CONV - 1 - j) * cw[j:j + 1]
        tail = up[tm - SUBLANES:]
        prev_ref[:, cols] = tail
        cstate_ref[0, :, cols] = tail
        return u

    chunks = [(c0, min(chunk_cols, d_ff - c0)) for c0 in range(0, d_ff, chunk_cols)]
    gates = [(conv_up(c0, cs), conv_up(d_ff + c0, cs)) for c0, cs in chunks[:1]]
    y = None
    for i, (c0, cs) in enumerate(chunks):
        if i + 1 < len(chunks):
            n0, ns = chunks[i + 1]
            gates.append((conv_up(n0, ns), conv_up(d_ff + n0, ns)))
        ua, ub = gates[i]
        part = _dot((_silu(ua) * ub).astype(BF16), wdn_ref[c0:c0 + cs, :])
        y = part if y is None else y + part
    y_ref[0] = _layer_norm(alpha * x1 + y, g2_ref[...], b2_ref[...])


def _ffn(x, oa, ob, w_o, g1, b1, w_up, cw, cb, cbuf8, w_dn, g2, b2, tm, alpha):
    B, T, D = x.shape
    nt = T // tm
    d_ff = w_dn.shape[0]
    tok_spec = lambda width: pl.BlockSpec((1, tm, width), lambda b, t: (b, t, 0))
    full = lambda a: pl.BlockSpec(a.shape, lambda b, t: (0,) * a.ndim)
    once = lambda a: pl.BlockSpec(a.shape, lambda b, t: (0,) * a.ndim,
                                  pipeline_mode=pl.Buffered(1))
    st_spec = pl.BlockSpec((1, SUBLANES, 2 * d_ff), lambda b, t: (b, 0, 0))
    return pl.pallas_call(
        functools.partial(_ffn_kernel, alpha=alpha, d_ff=d_ff, chunk_cols=FFN_CHUNK_COLS),
        grid=(B, nt),
        in_specs=[tok_spec(D), tok_spec(GDN_V), tok_spec(DIFF_V), once(w_o), full(g1), full(b1),
                  once(w_up), full(cw), full(cb), st_spec, once(w_dn), full(g2), full(b2)],
        out_specs=(tok_spec(D), st_spec),
        out_shape=(jax.ShapeDtypeStruct((B, T, D), F32),
                   jax.ShapeDtypeStruct((B, SUBLANES, 2 * d_ff), F32)),
        scratch_shapes=[pltpu.VMEM((SUBLANES, 2 * d_ff), F32)],
        compiler_params=pltpu.CompilerParams(
            dimension_semantics=("arbitrary", "arbitrary"),
            vmem_limit_bytes=VMEM_LIMIT_BYTES),
        name="ffn",
    )(x, oa, ob, w_o, g1, b1, w_up, cw, cb, cbuf8, w_dn, g2, b2)


def _pad_rows_front(buf, rows):
    B, r, C = buf.shape
    return jnp.concatenate([jnp.zeros((B, rows - r, C), buf.dtype), buf], axis=1)


def _lane_row(vec):
    n = vec.shape[0]
    pad = (-n) % LANES
    return jnp.pad(vec.astype(F32), (0, pad)).reshape(1, n + pad)


def _rope_tables(pos0, T):
    half = DIFF_DH // 2
    inv = ROPE_THETA ** (-jnp.arange(half, dtype=F32) * (2.0 / DIFF_DH))
    pos = pos0 + jnp.arange(T, dtype=jnp.int32)
    ang = pos.astype(F32)[:, None] * inv[None, :]
    cos = jnp.cos(ang)
    sin = jnp.sin(ang)
    reps = LANES // DIFF_DH
    cos_t = jnp.tile(jnp.concatenate([cos, cos], axis=-1), (1, reps))
    sin_t = jnp.tile(jnp.concatenate([-sin, sin], axis=-1), (1, reps))
    return cos_t, sin_t


def _largest_divisor(n, cap, mult):
    best = None
    for d in range(mult, min(n, cap) + 1, mult):
        if n % d == 0:
            best = d
    return best if best is not None else n


def _layer(x, pos0, k_past, v_past, s0, conv_qkv_buf, conv_ffn_buf, p, lam_init, alpha):
    B, T, _ = x.shape
    tm = _largest_divisor(T, 512, SUBLANES)
    cos_t, sin_t = _rope_tables(pos0, T)
    (qg, kg, vg, gate, gb, qd, kd, vd, k_new, v_new, cq8) = _inproj(
        x, p["w_in"], p["gdn_conv_w"], _pad_rows_front(conv_qkv_buf, SUBLANES),
        p["alog"], p["dtb"], cos_t, sin_t, tm)

    chunk = CHUNK if T % CHUNK == 0 else T
    n_chunks = _largest_divisor(T // chunk, 4, 1)
    oa, s_new = _gdn(qg, kg, vg, gate, gb, s0, p["gdn_norm_w"], chunk, n_chunks)

    if k_past is None:
        k_all, v_all = kd, vd
    else:
        k_all = jnp.concatenate([k_past.reshape(B, -1, DIFF_QK).astype(BF16), kd], axis=1)
        v_all = jnp.concatenate([v_past.reshape(B, -1, DIFF_V).astype(BF16), vd], axis=1)
    Tk = k_all.shape[1]
    bq = _largest_divisor(T, 512, 16)
    bk = _largest_divisor(Tk, 1536 if Tk % 512 else 512, 16)
    ob = _attn(qd, k_all, v_all, p["diff_lambda"], p["diff_subln_w"], bq, bk, 4, pos0, lam_init)

    tm_f = _largest_divisor(T, 512, SUBLANES)
    y, cf8 = _ffn(x, oa, ob, p["w_o"], p["ln1_g"], p["ln1_b"], p["w_up"], p["ffn_conv_w"],
                  p["ffn_conv_b"], _pad_rows_front(conv_ffn_buf, SUBLANES), p["w_down"],
                  p["ln2_g"], p["ln2_b"], tm_f, alpha)

    new_k = k_new.reshape(B, T, DIFF_HEADS, 2 * DIFF_DH)
    new_v = v_new.reshape(B, T, DIFF_HEADS, DIFF_DV)
    return (y, new_k, new_v, s_new, cq8[:, SUBLANES - (GDN_CONV - 1):],
            cf8[:, SUBLANES - (FFN_CONV - 1):])


def _prep_params(l, w_in, gdn_conv_w, gdn_a_log, gdn_dt_bias, gdn_norm_w, diff_lambda,
                 diff_subln_w, w_o, ln1_g, ln1_b, w_up, ffn_conv_w, ffn_conv_b, w_down,
                 ln2_g, ln2_b):
    o1 = GDN_QKV
    o2 = o1 + GDN_V
    o3 = o2 + GDN_HEADS
    o4 = o3 + GDN_HEADS
    w = w_in[l]
    d_model = w.shape[0]
    w_r = jnp.concatenate(
        [w[:, :o2], w[:, o4:], w[:, o2:o4],
         jnp.zeros((d_model, LANES - 2 * GDN_HEADS), w.dtype)], axis=1).astype(BF16)
    row = lambda v: v.astype(F32).reshape(1, -1)
    return {
        "w_in": w_r, "gdn_conv_w": gdn_conv_w[l].astype(F32),
        "alog": _lane_row(gdn_a_log[l]), "dtb": _lane_row(gdn_dt_bias[l]),
        "gdn_norm_w": row(gdn_norm_w[l]), "diff_lambda": diff_lambda[l].astype(F32),
        "diff_subln_w": row(diff_subln_w[l]), "w_o": w_o[l].astype(BF16),
        "ln1_g": row(ln1_g[l]), "ln1_b": row(ln1_b[l]), "w_up": w_up[l].astype(BF16),
        "ffn_conv_w": ffn_conv_w[l].astype(F32), "ffn_conv_b": row(ffn_conv_b[l]),
        "w_down": w_down[l].astype(BF16), "ln2_g": row(ln2_g[l]), "ln2_b": row(ln2_b[l]),
    }


def kernel(x_prompt, x_sample, cache_k, cache_v, state_gdn, state_conv_qkv, state_conv_ffn, w_in, gdn_conv_w, gdn_a_log, gdn_dt_bias, gdn_norm_w, diff_lambda, diff_subln_w, w_o, ln1_g, ln1_b, w_up, ffn_conv_w, ffn_conv_b, w_down, ln2_g, ln2_b):
    depth = w_in.shape[0]
    Bp = x_prompt.shape[0]
    past = cache_k.shape[2]
    alpha = (2 * depth) ** 0.25
    xp, xs = x_prompt, x_sample
    outs_p, outs_s = [], []
    for l in range(depth):
        p = _prep_params(l, w_in, gdn_conv_w, gdn_a_log, gdn_dt_bias, gdn_norm_w, diff_lambda,
                         diff_subln_w, w_o, ln1_g, ln1_b, w_up, ffn_conv_w, ffn_conv_b, w_down,
                         ln2_g, ln2_b)
        lam_init = 0.8 - 0.6 * math.exp(-0.3 * l)
        s0_p = jnp.zeros((Bp, GDN_HEADS, GDN_DK, GDN_DV), F32)
        cq0_p = jnp.zeros((Bp, GDN_CONV - 1, GDN_QKV), xp.dtype)
        cf0_p = jnp.zeros((Bp, FFN_CONV - 1, state_conv_ffn.shape[-1]), xp.dtype)
        xp, kp, vp, sp, cqp, cfp = _layer(xp, 0, None, None, s0_p, cq0_p, cf0_p, p, lam_init, alpha)
        xs, ksn, vsn, ssn, cqs, cfs = _layer(xs, past, cache_k[l], cache_v[l], state_gdn[l],
                                             state_conv_qkv[l], state_conv_ffn[l], p, lam_init,
                                             alpha)
        outs_p.append((kp, vp, sp, cqp, cfp))
        outs_s.append((ksn, vsn, ssn, cqs, cfs))

    def stk(outs, i):
        return jnp.stack([o[i] for o in outs])

    return (xp, xs,
            stk(outs_p, 0), stk(outs_p, 1), stk(outs_p, 2), stk(outs_p, 3), stk(outs_p, 4),
            stk(outs_s, 0), stk(outs_s, 1), stk(outs_s, 2), stk(outs_s, 3), stk(outs_s, 4))
```

```python
import functools
import math

import jax
import jax.numpy as jnp
from jax import lax
from jax.experimental import pallas as pl
from jax.experimental.pallas import tpu as pltpu

F32 = jnp.float32
BF16 = jnp.bfloat16

LANES = 128
SUBLANES = 8
VMEM_LIMIT_BYTES = 56 * 1024 * 1024
FFN_CHUNK_COLS = 2816
GDN_STEP_CHUNKS = 8
GDN_SET_CHUNKS = 2

CHUNK = 64
GDN_HEADS = 4
GDN_DK = 128
GDN_DV = 128
GDN_CONV = 4
DIFF_HEADS = 4
DIFF_DH = 64
DIFF_DV = 2 * DIFF_DH
ROPE_THETA = 10000.0
FFN_CONV = 3
LN_EPS = 1e-5
RMS_EPS = 1e-6
L2_EPS = 1e-6
NEG_INF = -1e30

GDN_QK = GDN_HEADS * GDN_DK
GDN_V = GDN_HEADS * GDN_DV
GDN_QKV = 2 * GDN_QK + GDN_V
DIFF_QK = DIFF_HEADS * 2 * DIFF_DH
DIFF_V = DIFF_HEADS * DIFF_DV

COL_GATE = GDN_QKV
COL_QB = COL_GATE + GDN_V
COL_KB = COL_QB + DIFF_QK
COL_VB = COL_KB + DIFF_QK
COL_AB = COL_VB + DIFF_V
D_IN_PAD = COL_AB + LANES


def _nt_dot(a, b):
    return lax.dot_general(a, b, (((1,), (1,)), ((), ())), preferred_element_type=F32)


def _tn_dot(a, b):
    return lax.dot_general(a, b, (((0,), (0,)), ((), ())), preferred_element_type=F32)


def _dot(a, b):
    return jnp.dot(a, b, preferred_element_type=F32)


def _sigmoid(x):
    return 1.0 / (1.0 + jnp.exp(-x))


def _silu(x):
    return x * _sigmoid(x)


def _softplus(x):
    return jnp.maximum(x, 0.0) + jnp.log(1.0 + jnp.exp(-jnp.abs(x)))


def _layer_norm(x, g, b):
    mu = jnp.mean(x, axis=-1, keepdims=True)
    xc = x - mu
    var = jnp.mean(xc * xc, axis=-1, keepdims=True)
    return xc * lax.rsqrt(var + LN_EPS) * g + b


def _shifted_rows(prev, cur, back):
    if back == 0:
        return cur
    ext = jnp.concatenate([prev, cur], axis=0)
    n = cur.shape[0]
    return ext[SUBLANES - back:SUBLANES - back + n]


def _inproj_kernel(x_ref, w_ref, convw_ref, cbuf_ref, alog_ref, dtb_ref, cos_ref, sin_ref,
                   qg_ref, kg_ref, vg_ref, gate_ref, gb_ref, qd_ref, kd_ref, vd_ref,
                   kout_ref, vout_ref, cstate_ref, prev_ref):
    t = pl.program_id(1)

    @pl.when(t == 0)
    def _():
        prev_ref[:SUBLANES, :] = cbuf_ref[0]

    tm = x_ref.shape[1]
    x = x_ref[0].astype(BF16)
    proj = lambda lo, hi: _dot(x, w_ref[:, lo:hi])

    def gdn_qkv(qkv):
        cw = convw_ref[...]
        prev_ref[SUBLANES:, :] = qkv
        y = qkv * cw[GDN_CONV - 1:GDN_CONV]
        for j in range(GDN_CONV - 1):
            back = GDN_CONV - 1 - j
            y = y + prev_ref[pl.ds(SUBLANES - back, tm), :] * cw[j:j + 1]
        tail = qkv[tm - SUBLANES:]
        prev_ref[:SUBLANES, :] = tail
        cstate_ref[0] = tail
        y = _silu(y)
        for hd in range(GDN_HEADS):
            for base, ref in ((0, qg_ref), (GDN_QK, kg_ref)):
                z = y[:, base + hd * GDN_DK: base + (hd + 1) * GDN_DK]
                z = z * lax.rsqrt(jnp.sum(z * z, axis=-1, keepdims=True) + L2_EPS)
                ref[0, :, hd * GDN_DK:(hd + 1) * GDN_DK] = z.astype(BF16)
        vg_ref[0] = y[:, 2 * GDN_QK:].astype(BF16)

    def rope(z):
        first_half = (lax.broadcasted_iota(jnp.int32, z.shape, 1) % DIFF_DH) < (DIFF_DH // 2)
        swapped = jnp.where(first_half, pltpu.roll(z, LANES - DIFF_DH // 2, 1),
                            pltpu.roll(z, DIFF_DH // 2, 1))
        return z * cos_ref[...] + swapped * sin_ref[...]

    head_rows = lambda hd: pl.ds(hd, tm, stride=DIFF_HEADS)

    def gate_and_q(hm):
        gate_ref[0] = _silu(hm[:, :GDN_V]).astype(BF16)
        scale = DIFF_DH ** -0.5 * math.log2(math.e)
        for hd in range(DIFF_HEADS):
            sl = slice(hd * LANES, (hd + 1) * LANES)
            qd_ref[0, :, sl] = (rope(hm[:, GDN_V + hd * LANES:GDN_V + (hd + 1) * LANES])
                                * scale).astype(BF16)

    def k_v_gates(hl):
        for hd in range(DIFF_HEADS):
            sl = slice(hd * LANES, (hd + 1) * LANES)
            r = rope(hl[:, sl])
            kout_ref[0, head_rows(hd), :] = r
            kd_ref[0, :, sl] = r.astype(BF16)
        vb = hl[:, DIFF_QK:DIFF_QK + DIFF_V]
        for hd in range(DIFF_HEADS):
            vout_ref[0, head_rows(hd), :] = vb[:, hd * DIFF_DV:(hd + 1) * DIFF_DV]
        vd_ref[0] = vb.astype(BF16)
        ab = hl[:, DIFF_QK + DIFF_V:]
        g = -jnp.exp(alog_ref[...]) * _softplus(ab + dtb_ref[...])
        lane = lax.broadcasted_iota(jnp.int32, ab.shape, 1)
        gb_ref[0] = jnp.where(lane < GDN_HEADS, g, _sigmoid(ab))

    h_qkv = proj(0, COL_GATE)
    h_mid = proj(COL_GATE, COL_KB)
    gdn_qkv(h_qkv)
    h_last = proj(COL_KB, D_IN_PAD)
    gate_and_q(h_mid)
    k_v_gates(h_last)


def _inproj(x, w_in_r, conv_w, cbuf8, alog, dtb, cos_t, sin_t, tm):
    B, T, D = x.shape
    nt = T // tm
    tok = lambda width, dt: jax.ShapeDtypeStruct((B, T, width), dt)
    tok_spec = lambda width: pl.BlockSpec((1, tm, width), lambda b, t: (b, t, 0))
    full = lambda a: pl.BlockSpec(a.shape, lambda b, t: (0,) * a.ndim)
    out_shape = (tok(GDN_QK, BF16), tok(GDN_QK, BF16), tok(GDN_V, BF16), tok(GDN_V, BF16),
                 tok(LANES, F32), tok(DIFF_QK, BF16), tok(DIFF_QK, BF16), tok(DIFF_V, BF16),
                 jax.ShapeDtypeStruct((B, T * DIFF_HEADS, 2 * DIFF_DH), F32),
                 jax.ShapeDtypeStruct((B, T * DIFF_HEADS, DIFF_DV), F32),
                 jax.ShapeDtypeStruct((B, SUBLANES, GDN_QKV), F32))
    out_specs = (tok_spec(GDN_QK), tok_spec(GDN_QK), tok_spec(GDN_V), tok_spec(GDN_V),
                 tok_spec(LANES), tok_spec(DIFF_QK), tok_spec(DIFF_QK), tok_spec(DIFF_V),
                 pl.BlockSpec((1, tm * DIFF_HEADS, 2 * DIFF_DH), lambda b, t: (b, t, 0)),
                 pl.BlockSpec((1, tm * DIFF_HEADS, DIFF_DV), lambda b, t: (b, t, 0)),
                 pl.BlockSpec((1, SUBLANES, GDN_QKV), lambda b, t: (b, 0, 0)))
    in_specs = [tok_spec(D), full(w_in_r), full(conv_w),
                pl.BlockSpec((1, SUBLANES, GDN_QKV), lambda b, t: (b, 0, 0)),
                full(alog), full(dtb),
                pl.BlockSpec((tm, LANES), lambda b, t: (t, 0)),
                pl.BlockSpec((tm, LANES), lambda b, t: (t, 0))]
    return pl.pallas_call(
        _inproj_kernel,
        grid=(B, nt),
        in_specs=in_specs,
        out_specs=out_specs,
        out_shape=out_shape,
        scratch_shapes=[pltpu.VMEM((SUBLANES + tm, GDN_QKV), F32)],
        compiler_params=pltpu.CompilerParams(
            dimension_semantics=("arbitrary", "arbitrary"),
            vmem_limit_bytes=VMEM_LIMIT_BYTES),
        name="inproj",
    )(x, w_in_r, conv_w, cbuf8, alog, dtb, cos_t, sin_t)


def _split_hi_lo(x):
    hi = x.astype(BF16)
    return hi, (x - hi.astype(F32)).astype(BF16)


def _blockdiag_rows(x, width):
    n = x.shape[1] // width
    blk = lax.broadcasted_iota(jnp.int32, x.shape, 1) // width
    zero = jnp.zeros_like(x)
    return jnp.concatenate([jnp.where(blk == j, x, zero) for j in range(n)], axis=0)


def _packed_dot_split(l2, r2, width):
    l_hi, l_lo = _split_hi_lo(l2)
    r_hi, r_lo = _split_hi_lo(r2)
    bd_hi = _blockdiag_rows(r_hi, width)
    bd_lo = _blockdiag_rows(r_lo, width)
    lhs = jnp.concatenate([l_hi, l_lo, l_hi, l_lo], axis=1)
    rhs = jnp.concatenate([bd_hi, bd_hi, bd_lo, bd_lo], axis=0)
    return _dot(lhs, rhs)


def _per_head(cols, lane_head):
    out = cols[0]
    for j in range(1, len(cols)):
        out = jnp.where(lane_head >= j, cols[j], out)
    return out


def _gdn_kernel(q_ref, k_ref, v_ref, gate_ref, gb_ref, s0_ref, nw_ref,
                o_ref, sout_ref, s_ref, *, chunk, n_chunks, set_chunks):
    t = pl.program_id(1)

    @pl.when(t == 0)
    def _():
        s_ref[...] = s0_ref[0]

    C = chunk
    G = LANES // C
    W = G * GDN_DK
    n_groups = GDN_HEADS // G
    row = lax.broadcasted_iota(jnp.int32, (C, LANES), 0)
    lane = lax.broadcasted_iota(jnp.int32, (C, LANES), 1)
    lane_in = lane % C
    lane_head_p = lane // C
    lane_head_w = lax.broadcasted_iota(jnp.int32, (C, W), 1) // GDN_DK
    incl = row >= lane_in
    strict = row > lane_in
    eye_b = row == lane_in
    eye = eye_b.astype(F32)
    r1 = lax.broadcasted_iota(jnp.int32, (C, C), 0)
    c1 = lax.broadcasted_iota(jnp.int32, (C, C), 1)
    tri = (r1 >= c1).astype(BF16)
    tri3 = jnp.concatenate([tri, tri, tri], axis=1)
    scale = GDN_DK ** -0.5
    nw = nw_ref[...]
    n_sq = max(int(math.ceil(math.log2(C))) - 1, 0)

    head_slices = [slice(j * GDN_DK, (j + 1) * GDN_DK) for j in range(G)]

    def local_part(chunks, out):
        items = []
        for c in chunks:
            rows = slice(c * C, (c + 1) * C)
            gbc = gb_ref[0, rows, :]
            g_hi = gbc.astype(BF16)
            g_rest = gbc - g_hi.astype(F32)
            g_mid = g_rest.astype(BF16)
            g_lo = (g_rest - g_mid.astype(F32)).astype(BF16)
            gc_all = _dot(tri3, jnp.concatenate([g_hi, g_mid, g_lo], axis=0))
            for gi in range(n_groups):
                heads = [gi * G + j for j in range(G)]
                cols = slice(gi * W, (gi + 1) * W)
                gcols = [gc_all[:, h:h + 1] for h in heads]
                bcols = [gbc[:, GDN_HEADS + h:GDN_HEADS + h + 1] for h in heads]
                gcol_p = _per_head(gcols, lane_head_p)
                grow_p = jnp.sum(jnp.where(eye_b, gcol_p, 0.0), axis=0, keepdims=True)
                decay = jnp.exp(jnp.where(incl, gcol_p - grow_p, -jnp.inf))
                b_w = _per_head(bcols, lane_head_w)
                eg_w = _per_head([jnp.exp(g) for g in gcols], lane_head_w)
                q = q_ref[0, rows, cols].astype(F32) * scale
                k = k_ref[0, rows, cols].astype(F32)
                v = v_ref[0, rows, cols].astype(F32)
                kb = k * b_w
                items.append(dict(c=c, q=q, k=k, kb=kb, vb=v * b_w, kbe=kb * eg_w, qe=q * eg_w,
                                  decay=decay, gcols=gcols))
        yield
        for it in items:
            k_bd = _blockdiag_rows(it["k"].astype(BF16), GDN_DK)
            aq = _nt_dot(jnp.concatenate([it["kb"].astype(BF16), it["q"].astype(BF16)], axis=0),
                         k_bd)
            it["qk"] = aq[C:] * it["decay"]
            it["p"] = -jnp.where(strict, aq[:C] * it["decay"], 0.0)
            it["t"] = eye + it["p"]
        yield
        if n_sq > 0:
            for it in items:
                it["p"] = _packed_dot_split(it["p"], it["p"], C)
            yield
            for _ in range(n_sq - 1):
                for it in items:
                    both = _packed_dot_split(jnp.concatenate([it["t"], it["p"]], axis=0),
                                             it["p"], C)
                    it["t"] = it["t"] + both[:C]
                    it["p"] = both[C:]
                yield
            for it in items:
                it["t"] = it["t"] + _packed_dot_split(it["t"], it["p"], C)
            yield
        for it in items:
            rhs = jnp.concatenate([_blockdiag_rows(it["vb"].astype(BF16), GDN_DV),
                                   _blockdiag_rows(it["kbe"].astype(BF16), GDN_DK)], axis=1)
            uw = _dot(it["t"].astype(BF16), rhs)
            out.setdefault(it["c"], []).append(
                (uw[:, :W], uw[:, W:], it["qk"], it["qe"], it["k"], it["gcols"]))

    state = [s_ref[h] for h in range(GDN_HEADS)]

    def recurrence(chunks, local):
        for c in chunks:
            rows = slice(c * C, (c + 1) * C)
            groups = local[c]
            r = [[_dot(jnp.concatenate([w[:, hs], qe[:, hs]], axis=0).astype(BF16),
                       state[gi * G + j].astype(BF16))
                  for j, hs in enumerate(head_slices)]
                 for gi, (u, w, qk, qe, k, gcols) in enumerate(groups)]
            yield
            v_new_bf = [[(u[:, hs] - r[gi][j][:C]).astype(BF16)
                         for j, hs in enumerate(head_slices)]
                        for gi, (u, w, qk, qe, k, gcols) in enumerate(groups)]
            o = [jnp.concatenate([r[gi][j][C:] for j in range(G)], axis=1)
                 + _dot(qk.astype(BF16),
                        _blockdiag_rows(jnp.concatenate(v_new_bf[gi], axis=1), GDN_DV))
                 for gi, (u, w, qk, qe, k, gcols) in enumerate(groups)]
            for gi, (u, w, qk, qe, k, gcols) in enumerate(groups):
                for j, hs in enumerate(head_slices):
                    h = gi * G + j
                    g_last = gcols[j][C - 1:C, :]
                    k_dec = (k[:, hs] * jnp.exp(g_last - gcols[j])).astype(BF16)
                    state[h] = state[h] * jnp.exp(g_last) + _tn_dot(k_dec, v_new_bf[gi][j])
            yield
            for gi in range(n_groups):
                for j, hs in enumerate(head_slices):
                    h = gi * G + j
                    oh = o[gi][:, hs]
                    on = oh * lax.rsqrt(jnp.mean(oh * oh, axis=-1, keepdims=True) + RMS_EPS) * nw
                    ocols = slice(h * GDN_DV, (h + 1) * GDN_DV)
                    o_ref[0, rows, ocols] = (on * gate_ref[0, rows, ocols].astype(F32)
                                             ).astype(BF16)
            yield

    def alternate(*gens):
        gens = list(gens)
        done = object()
        while gens:
            for g in list(gens):
                if next(g, done) is done:
                    gens.remove(g)

    sets = [list(range(c0, min(c0 + set_chunks, n_chunks))) for c0 in range(0, n_chunks, set_chunks)]
    local = {}
    alternate(local_part(sets[0], local))
    for prev, cur in zip(sets[:-1], sets[1:]):
        alternate(local_part(cur, local), recurrence(prev, local))
    alternate(recurrence(sets[-1], local))

    for h in range(GDN_HEADS):
        s_ref[h] = state[h]
        sout_ref[0, h] = state[h]


def _gdn(qg, kg, vg, gate, gb, s0, norm_w, chunk, n_chunks):
    B, T, _ = qg.shape
    tb = chunk * n_chunks
    nt = T // tb
    tok_spec = lambda width: pl.BlockSpec((1, tb, width), lambda b, t: (b, t, 0))
    st_spec = pl.BlockSpec((1, GDN_HEADS, GDN_DK, GDN_DV), lambda b, t: (b, 0, 0, 0))
    return pl.pallas_call(
        functools.partial(_gdn_kernel, chunk=chunk, n_chunks=n_chunks,
                          set_chunks=GDN_SET_CHUNKS),
        grid=(B, nt),
        in_specs=[tok_spec(GDN_QK), tok_spec(GDN_QK), tok_spec(GDN_V), tok_spec(GDN_V),
                  tok_spec(LANES), st_spec, pl.BlockSpec((1, GDN_DV), lambda b, t: (0, 0))],
        out_specs=(tok_spec(GDN_V), st_spec),
        out_shape=(jax.ShapeDtypeStruct((B, T, GDN_V), BF16),
                   jax.ShapeDtypeStruct((B, GDN_HEADS, GDN_DK, GDN_DV), F32)),
        scratch_shapes=[pltpu.VMEM((GDN_HEADS, GDN_DK, GDN_DV), F32)],
        compiler_params=pltpu.CompilerParams(
            dimension_semantics=("arbitrary", "arbitrary"),
            vmem_limit_bytes=VMEM_LIMIT_BYTES),
        name="gdn",
    )(qg, kg, vg, gate, gb, s0, norm_w)


def _lane_tile(x, width):
    if width % LANES == 0:
        return jnp.concatenate([x] * (width // LANES), axis=1)
    return x[:, :1]


def _attn_kernel(*refs, bq, bk, cache_bk, unroll, q_pos0, k_pos0, lam_init):
    if cache_bk:
        q_ref, k_ref, v_ref, kc_ref, vc_ref, lam_ref, w_ref, o_ref, m_ref, acc_ref = refs
    else:
        q_ref, k_ref, v_ref, lam_ref, w_ref, o_ref, m_ref, acc_ref = refs
    head = pl.program_id(1)
    qi = pl.program_id(2)
    n_blocks = k_ref.shape[1] // bk

    q = q_ref[0]
    lane = lax.broadcasted_iota(jnp.int32, q.shape, 1)
    zero = jnp.zeros_like(q)
    q_maps = [jnp.where(lane < DIFF_DH, q, zero), jnp.where(lane >= DIFF_DH, q, zero)]

    q_first = q_pos0 + qi * bq
    q_last = q_first + (bq - 1)
    vis_first = (q_first // CHUNK + 1) * CHUNK - k_pos0
    vis_last = (q_last // CHUNK + 1) * CHUNK - k_pos0
    n_full = jnp.clip(vis_first // bk, 0, n_blocks)
    n_all = jnp.clip((vis_last + bk - 1) // bk, 0, n_blocks)

    m_ref[...] = jnp.full(m_ref.shape, NEG_INF, F32)
    acc_ref[...] = jnp.zeros(acc_ref.shape, F32)

    def load_new(blk, width, ref):
        return ref[0, pl.ds(pl.multiple_of(blk * width, width), width), :].astype(BF16)

    def load_cache(blk, width, ref):
        rows = pl.ds(blk * width * DIFF_HEADS + head, width, stride=DIFF_HEADS)
        return ref[0, rows, :].astype(BF16)

    def steps(i, carry, masked, n, first, width, load, kr, vr):
        blocks = [first + i * n + j for j in range(n)]
        scores = [[_nt_dot(qm, load(blk, width, kr)) for qm in q_maps] for blk in blocks]
        ones = jnp.ones((width, LANES), BF16)
        m = [m_ref[mp] for mp in range(2)]
        acc = [acc_ref[mp] for mp in range(2)]
        for blk, s_maps in zip(blocks, scores):
            v_ext = jnp.concatenate([load(blk, width, vr), ones], axis=1)
            for mp, s in enumerate(s_maps):
                if masked:
                    kpos = k_pos0 + blk * width + lax.broadcasted_iota(jnp.int32, s.shape, 1)
                    qpos = q_first + lax.broadcasted_iota(jnp.int32, s.shape, 0)
                    s = jnp.where(kpos // CHUNK <= qpos // CHUNK, s, NEG_INF)
                m_new = jnp.maximum(m[mp], jnp.max(s, axis=-1, keepdims=True))
                p = jnp.exp2(s - _lane_tile(m_new, width))
                alpha = jnp.exp2(m[mp] - m_new)
                acc[mp] = (jnp.concatenate([alpha, alpha], axis=1) * acc[mp]
                           + _dot(p.astype(BF16), v_ext))
                m[mp] = m_new
        for mp in range(2):
            m_ref[mp] = m[mp]
            acc_ref[mp] = acc[mp]
        return carry

    def unmasked_run(count, width, load, kr, vr):
        done = 0
        u = unroll
        while u >= 1:
            trips = (count - done) // u
            lax.fori_loop(0, trips, functools.partial(steps, masked=False, n=u, first=done,
                                                      width=width, load=load, kr=kr, vr=vr), 0)
            done = done + trips * u
            u //= 2

    if cache_bk:
        unmasked_run(kc_ref.shape[1] // (cache_bk * DIFF_HEADS), cache_bk, load_cache, kc_ref,
                     vc_ref)
    unmasked_run(n_full, bk, load_new, k_ref, v_ref)
    lax.fori_loop(0, n_all - n_full,
                  functools.partial(steps, masked=True, n=1, first=n_full, width=bk, load=load_new,
                                    kr=k_ref, vr=v_ref), 0)

    lv = lam_ref[...]
    lam = (jnp.exp(jnp.sum(lv[0:1] * lv[1:2], axis=-1, keepdims=True))
           - jnp.exp(jnp.sum(lv[2:3] * lv[3:4], axis=-1, keepdims=True)) + lam_init)
    o = [acc_ref[mp, :, :DIFF_DV] / acc_ref[mp, :, DIFF_DV:] for mp in range(2)]
    out = o[0] - lam * o[1]
    out = out * lax.rsqrt(jnp.mean(out * out, axis=-1, keepdims=True) + RMS_EPS) * w_ref[...]
    o_ref[0] = (out * (1.0 - lam_init)).astype(BF16)


def _attn(qd, k_new, v_new, k_cache, v_cache, lam_p, subln_w, bq, bk, cache_bk, unroll, q_pos0,
          lam_init):
    B, Tq, _ = qd.shape
    Tk = k_new.shape[1]
    nq = Tq // bq
    kv_spec = pl.BlockSpec((1, Tk, LANES), lambda b, h, i: (b, 0, h))
    in_specs = [pl.BlockSpec((1, bq, LANES), lambda b, h, i: (b, i, h)), kv_spec, kv_spec]
    args = [qd, k_new, v_new]
    k_pos0 = 0
    if k_cache is not None:
        cache_spec = pl.BlockSpec((1,) + k_cache.shape[1:], lambda b, h, i: (b, 0, 0))
        in_specs += [cache_spec, cache_spec]
        args += [k_cache, v_cache]
        k_pos0 = k_cache.shape[1] // DIFF_HEADS
    in_specs += [pl.BlockSpec(lam_p.shape, lambda b, h, i: (0, 0)),
                 pl.BlockSpec(subln_w.shape, lambda b, h, i: (0, 0))]
    args += [lam_p, subln_w]
    return pl.pallas_call(
        functools.partial(_attn_kernel, bq=bq, bk=bk,
                          cache_bk=cache_bk if k_cache is not None else 0, unroll=unroll,
                          q_pos0=q_pos0, k_pos0=k_pos0, lam_init=lam_init),
        grid=(B, DIFF_HEADS, nq),
        in_specs=in_specs,
        out_specs=pl.BlockSpec((1, bq, LANES), lambda b, h, i: (b, i, h)),
        out_shape=jax.ShapeDtypeStruct((B, Tq, DIFF_V), BF16),
        scratch_shapes=[pltpu.VMEM((2, bq, LANES), F32),
                        pltpu.VMEM((2, bq, DIFF_DV + LANES), F32)],
        compiler_params=pltpu.CompilerParams(
            dimension_semantics=("arbitrary", "arbitrary", "arbitrary"),
            vmem_limit_bytes=VMEM_LIMIT_BYTES),
        name="diffattn",
    )(*args)


def _ffn_kernel(x_ref, oa_ref, ob_ref, wo_ref, g1_ref, b1_ref, wup_ref, cw_ref, cb_ref, cbuf_ref,
                wdn_ref, g2_ref, b2_ref, y_ref, cstate_ref, prev_ref, *, alpha, d_ff,
                chunk_cols):
    t = pl.program_id(1)

    @pl.when(t == 0)
    def _():
        prev_ref[...] = cbuf_ref[0]

    tm = x_ref.shape[1]
    mix = _dot(oa_ref[0], wo_ref[:GDN_V, :]) + _dot(ob_ref[0], wo_ref[GDN_V:, :])
    x1 = _layer_norm(alpha * x_ref[0] + mix, g1_ref[...], b1_ref[...])
    x1b = x1.astype(BF16)

    def conv_up(col, width):
        cols = slice(col, col + width)
        up = _dot(x1b, wup_ref[:, cols])
        prev = prev_ref[:, cols]
        cw = cw_ref[:, cols]
        u = up * cw[FFN_CONV - 1:FFN_CONV] + cb_ref[:, cols]
        for j in range(FFN_CONV - 1):
            u = u + _shifted_rows(prev, up, FFN_CONV - 1 - j) * cw[j:j + 1]
        tail = up[tm - SUBLANES:]
        prev_ref[:, cols] = tail
        cstate_ref[0, :, cols] = tail
        return u

    chunks = [(c0, min(chunk_cols, d_ff - c0)) for c0 in range(0, d_ff, chunk_cols)]
    gates = [(conv_up(c0, cs), conv_up(d_ff + c0, cs)) for c0, cs in chunks[:1]]
    y = None
    for i, (c0, cs) in enumerate(chunks):
        if i + 1 < len(chunks):
            n0, ns = chunks[i + 1]
            gates.append((conv_up(n0, ns), conv_up(d_ff + n0, ns)))
        ua, ub = gates[i]
        part = _dot((_silu(ua) * ub).astype(BF16), wdn_ref[c0:c0 + cs, :])
        y = part if y is None else y + part
    y_ref[0] = _layer_norm(alpha * x1 + y, g2_ref[...], b2_ref[...])


def _ffn(x, oa, ob, w_o, g1, b1, w_up, cw, cb, cbuf8, w_dn, g2, b2, tm, alpha):
    B, T, D = x.shape
    nt = T // tm
    d_ff = w_dn.shape[0]
    tok_spec = lambda width: pl.BlockSpec((1, tm, width), lambda b, t: (b, t, 0))
    full = lambda a: pl.BlockSpec(a.shape, lambda b, t: (0,) * a.ndim)
    once = lambda a: pl.BlockSpec(a.shape, lambda b, t: (0,) * a.ndim,
                                  pipeline_mode=pl.Buffered(1))
    st_spec = pl.BlockSpec((1, SUBLANES, 2 * d_ff), lambda b, t: (b, 0, 0))
    return pl.pallas_call(
        functools.partial(_ffn_kernel, alpha=alpha, d_ff=d_ff, chunk_cols=FFN_CHUNK_COLS),
        grid=(B, nt),
        in_specs=[tok_spec(D), tok_spec(GDN_V), tok_spec(DIFF_V), once(w_o), full(g1), full(b1),
                  once(w_up), full(cw), full(cb), st_spec, once(w_dn), full(g2), full(b2)],
        out_specs=(tok_spec(D), st_spec),
        out_shape=(jax.ShapeDtypeStruct((B, T, D), F32),
                   jax.ShapeDtypeStruct((B, SUBLANES, 2 * d_ff), F32)),
        scratch_shapes=[pltpu.VMEM((SUBLANES, 2 * d_ff), F32)],
        compiler_params=pltpu.CompilerParams(
            dimension_semantics=("arbitrary", "arbitrary"),
            vmem_limit_bytes=VMEM_LIMIT_BYTES),
        name="ffn",
    )(x, oa, ob, w_o, g1, b1, w_up, cw, cb, cbuf8, w_dn, g2, b2)


def _pad_rows_front(buf, rows):
    B, r, C = buf.shape
    return jnp.concatenate([jnp.zeros((B, rows - r, C), buf.dtype), buf], axis=1)


def _lane_row(vec):
    n = vec.shape[0]
    pad = (-n) % LANES
    return jnp.pad(vec.astype(F32), (0, pad)).reshape(1, n + pad)


def _rope_tables(pos0, T):
    half = DIFF_DH // 2
    inv = ROPE_THETA ** (-jnp.arange(half, dtype=F32) * (2.0 / DIFF_DH))
    pos = pos0 + jnp.arange(T, dtype=jnp.int32)
    ang = pos.astype(F32)[:, None] * inv[None, :]
    cos = jnp.cos(ang)
    sin = jnp.sin(ang)
    reps = LANES // DIFF_DH
    cos_t = jnp.tile(jnp.concatenate([cos, cos], axis=-1), (1, reps))
    sin_t = jnp.tile(jnp.concatenate([-sin, sin], axis=-1), (1, reps))
    return cos_t, sin_t


def _largest_divisor(n, cap, mult):
    best = None
    for d in range(mult, min(n, cap) + 1, mult):
        if n % d == 0:
            best = d
    return best if best is not None else n


def _layer(x, pos0, k_past, v_past, s0, conv_qkv_buf, conv_ffn_buf, p, lam_init, alpha):
    B, T, _ = x.shape
    tm = _largest_divisor(T, 512, SUBLANES)
    cos_t, sin_t = _rope_tables(pos0, T)
    (qg, kg, vg, gate, gb, qd, kd, vd, k_new, v_new, cq8) = _inproj(
        x, p["w_in"], p["gdn_conv_w"], _pad_rows_front(conv_qkv_buf, SUBLANES),
        p["alog"], p["dtb"], cos_t, sin_t, tm)

    chunk = CHUNK if T % CHUNK == 0 else T
    n_chunks = _largest_divisor(T // chunk, GDN_STEP_CHUNKS, 1)
    oa, s_new = _gdn(qg, kg, vg, gate, gb, s0, p["gdn_norm_w"], chunk, n_chunks)

    if k_past is None:
        k_cache = v_cache = None
        cache_bk = 0
    else:
        k_cache = k_past.reshape(B, -1, 2 * DIFF_DH)
        v_cache = v_past.reshape(B, -1, DIFF_DV)
        cache_bk = _largest_divisor(k_past.shape[1], 512, 16)
    bq = _largest_divisor(T, 512, 16)
    bk = _largest_divisor(T, 512, 16)
    ob = _attn(qd, kd, vd, k_cache, v_cache, p["diff_lambda"], p["diff_subln_w"], bq, bk,
               cache_bk, 4, pos0, lam_init)

    tm_f = _largest_divisor(T, 512, SUBLANES)
    y, cf8 = _ffn(x, oa, ob, p["w_o"], p["ln1_g"], p["ln1_b"], p["w_up"], p["ffn_conv_w"],
                  p["ffn_conv_b"], _pad_rows_front(conv_ffn_buf, SUBLANES), p["w_down"],
                  p["ln2_g"], p["ln2_b"], tm_f, alpha)

    new_k = k_new.reshape(B, T, DIFF_HEADS, 2 * DIFF_DH)
    new_v = v_new.reshape(B, T, DIFF_HEADS, DIFF_DV)
    return (y, new_k, new_v, s_new, cq8[:, SUBLANES - (GDN_CONV - 1):],
            cf8[:, SUBLANES - (FFN_CONV - 1):])


def _prep_params(l, w_in, gdn_conv_w, gdn_a_log, gdn_dt_bias, gdn_norm_w, diff_lambda,
                 diff_subln_w, w_o, ln1_g, ln1_b, w_up, ffn_conv_w, ffn_conv_b, w_down,
                 ln2_g, ln2_b):
    o1 = GDN_QKV
    o2 = o1 + GDN_V
    o3 = o2 + GDN_HEADS
    o4 = o3 + GDN_HEADS
    w = w_in[l]
    d_model = w.shape[0]
    w_r = jnp.concatenate(
        [w[:, :o2], w[:, o4:], w[:, o2:o4],
         jnp.zeros((d_model, LANES - 2 * GDN_HEADS), w.dtype)], axis=1).astype(BF16)
    row = lambda v: v.astype(F32).reshape(1, -1)
    return {
        "w_in": w_r, "gdn_conv_w": gdn_conv_w[l].astype(F32),
        "alog": _lane_row(gdn_a_log[l]), "dtb": _lane_row(gdn_dt_bias[l]),
        "gdn_norm_w": row(gdn_norm_w[l]), "diff_lambda": diff_lambda[l].astype(F32),
        "diff_subln_w": row(diff_subln_w[l]), "w_o": w_o[l].astype(BF16),
        "ln1_g": row(ln1_g[l]), "ln1_b": row(ln1_b[l]), "w_up": w_up[l].astype(BF16),
        "ffn_conv_w": ffn_conv_w[l].astype(F32), "ffn_conv_b": row(ffn_conv_b[l]),
        "w_down": w_down[l].astype(BF16), "ln2_g": row(ln2_g[l]), "ln2_b": row(ln2_b[l]),
    }


def kernel(x_prompt, x_sample, cache_k, cache_v, state_gdn, state_conv_qkv, state_conv_ffn, w_in, gdn_conv_w, gdn_a_log, gdn_dt_bias, gdn_norm_w, diff_lambda, diff_subln_w, w_o, ln1_g, ln1_b, w_up, ffn_conv_w, ffn_conv_b, w_down, ln2_g, ln2_b):
    depth = w_in.shape[0]
    Bp = x_prompt.shape[0]
    past = cache_k.shape[2]
    alpha = (2 * depth) ** 0.25
    xp, xs = x_prompt, x_sample
    outs_p, outs_s = [], []
    for l in range(depth):
        p = _prep_params(l, w_in, gdn_conv_w, gdn_a_log, gdn_dt_bias, gdn_norm_w, diff_lambda,
                         diff_subln_w, w_o, ln1_g, ln1_b, w_up, ffn_conv_w, ffn_conv_b, w_down,
                         ln2_g, ln2_b)
        lam_init = 0.8 - 0.6 * math.exp(-0.3 * l)
        s0_p = jnp.zeros((Bp, GDN_HEADS, GDN_DK, GDN_DV), F32)
        cq0_p = jnp.zeros((Bp, GDN_CONV - 1, GDN_QKV), xp.dtype)
        cf0_p = jnp.zeros((Bp, FFN_CONV - 1, state_conv_ffn.shape[-1]), xp.dtype)
        xp, kp, vp, sp, cqp, cfp = _layer(xp, 0, None, None, s0_p, cq0_p, cf0_p, p, lam_init, alpha)
        xs, ksn, vsn, ssn, cqs, cfs = _layer(xs, past, cache_k[l], cache_v[l], state_gdn[l],
                                             state_conv_qkv[l], state_conv_ffn[l], p, lam_init,
                                             alpha)
        outs_p.append((kp, vp, sp, cqp, cfp))
        outs_s.append((ksn, vsn, ssn, cqs, cfs))

    def stk(outs, i):
        return jnp.stack([o[i] for o in outs])

    return (xp, xs,
            stk(outs_p, 0), stk(outs_p, 1), stk(outs_p, 2), stk(outs_p, 3), stk(outs_p, 4),
            stk(outs_s, 0), stk(outs_s, 1), stk(outs_s, 2), stk(outs_s, 3), stk(outs_s, 4))
```

```python
import functools
import math

import jax
import jax.numpy as jnp
from jax import lax
from jax.experimental import pallas as pl
from jax.experimental.pallas import tpu as pltpu

F32 = jnp.float32
BF16 = jnp.bfloat16

LANES = 128
SUBLANES = 8
VMEM_LIMIT_BYTES = 56 * 1024 * 1024
FFN_CHUNK_COLS = 2816
GDN_STEP_CHUNKS = 8
GDN_SET_CHUNKS = 2

CHUNK = 64
GDN_HEADS = 4
GDN_DK = 128
GDN_DV = 128
GDN_CONV = 4
DIFF_HEADS = 4
DIFF_DH = 64
DIFF_DV = 2 * DIFF_DH
ROPE_THETA = 10000.0
FFN_CONV = 3
LN_EPS = 1e-5
RMS_EPS = 1e-6
L2_EPS = 1e-6
NEG_INF = -1e30

GDN_QK = GDN_HEADS * GDN_DK
GDN_V = GDN_HEADS * GDN_DV
GDN_QKV = 2 * GDN_QK + GDN_V
DIFF_QK = DIFF_HEADS * 2 * DIFF_DH
DIFF_V = DIFF_HEADS * DIFF_DV

COL_GATE = GDN_QKV
COL_QB = COL_GATE + GDN_V
COL_KB = COL_QB + DIFF_QK
COL_VB = COL_KB + DIFF_QK
COL_AB = COL_VB + DIFF_V
D_IN_PAD = COL_AB + LANES


def _nt_dot(a, b):
    return lax.dot_general(a, b, (((1,), (1,)), ((), ())), preferred_element_type=F32)


def _tn_dot(a, b):
    return lax.dot_general(a, b, (((0,), (0,)), ((), ())), preferred_element_type=F32)


def _dot(a, b):
    return jnp.dot(a, b, preferred_element_type=F32)


def _sigmoid(x):
    return 1.0 / (1.0 + jnp.exp(-x))


def _silu(x):
    return x * _sigmoid(x)


def _softplus(x):
    return jnp.maximum(x, 0.0) + jnp.log(1.0 + jnp.exp(-jnp.abs(x)))


def _layer_norm(x, g, b):
    mu = jnp.mean(x, axis=-1, keepdims=True)
    xc = x - mu
    var = jnp.mean(xc * xc, axis=-1, keepdims=True)
    return xc * lax.rsqrt(var + LN_EPS) * g + b


def _shifted_rows(prev, cur, back):
    if back == 0:
        return cur
    ext = jnp.concatenate([prev, cur], axis=0)
    n = cur.shape[0]
    return ext[SUBLANES - back:SUBLANES - back + n]


def _inproj_kernel(x_ref, w_ref, convw_ref, cbuf_ref, alog_ref, dtb_ref, cos_ref, sin_ref,
                   qg_ref, kg_ref, vg_ref, gate_ref, gb_ref, qd_ref, kd_ref, vd_ref,
                   kout_ref, vout_ref, cstate_ref, prev_ref):
    t = pl.program_id(1)

    @pl.when(t == 0)
    def _():
        prev_ref[:SUBLANES, :] = cbuf_ref[0]

    tm = x_ref.shape[1]
    x = x_ref[0].astype(BF16)
    proj = lambda lo, hi: _dot(x, w_ref[:, lo:hi])

    def gdn_qkv(qkv):
        cw = convw_ref[...]
        prev_ref[SUBLANES:, :] = qkv
        y = qkv * cw[GDN_CONV - 1:GDN_CONV]
        for j in range(GDN_CONV - 1):
            back = GDN_CONV - 1 - j
            y = y + prev_ref[pl.ds(SUBLANES - back, tm), :] * cw[j:j + 1]
        tail = qkv[tm - SUBLANES:]
        prev_ref[:SUBLANES, :] = tail
        cstate_ref[0] = tail
        y = _silu(y)
        for hd in range(GDN_HEADS):
            for base, ref in ((0, qg_ref), (GDN_QK, kg_ref)):
                z = y[:, base + hd * GDN_DK: base + (hd + 1) * GDN_DK]
                z = z * lax.rsqrt(jnp.sum(z * z, axis=-1, keepdims=True) + L2_EPS)
                ref[0, :, hd * GDN_DK:(hd + 1) * GDN_DK] = z.astype(BF16)
        vg_ref[0] = y[:, 2 * GDN_QK:].astype(BF16)

    def rope(z):
        first_half = (lax.broadcasted_iota(jnp.int32, z.shape, 1) % DIFF_DH) < (DIFF_DH // 2)
        swapped = jnp.where(first_half, pltpu.roll(z, LANES - DIFF_DH // 2, 1),
                            pltpu.roll(z, DIFF_DH // 2, 1))
        return z * cos_ref[...] + swapped * sin_ref[...]

    head_rows = lambda hd: pl.ds(hd, tm, stride=DIFF_HEADS)

    def gate_and_q(hm):
        gate_ref[0] = _silu(hm[:, :GDN_V]).astype(BF16)
        scale = DIFF_DH ** -0.5 * math.log2(math.e)
        for hd in range(DIFF_HEADS):
            sl = slice(hd * LANES, (hd + 1) * LANES)
            qd_ref[0, :, sl] = (rope(hm[:, GDN_V + hd * LANES:GDN_V + (hd + 1) * LANES])
                                * scale).astype(BF16)

    def k_v_gates(hl):
        for hd in range(DIFF_HEADS):
            sl = slice(hd * LANES, (hd + 1) * LANES)
            r = rope(hl[:, sl])
            kout_ref[0, head_rows(hd), :] = r
            kd_ref[0, :, sl] = r.astype(BF16)
        vb = hl[:, DIFF_QK:DIFF_QK + DIFF_V]
        for hd in range(DIFF_HEADS):
            vout_ref[0, head_rows(hd), :] = vb[:, hd * DIFF_DV:(hd + 1) * DIFF_DV]
        vd_ref[0] = vb.astype(BF16)
        ab = hl[:, DIFF_QK + DIFF_V:]
        g = -jnp.exp(alog_ref[...]) * _softplus(ab + dtb_ref[...])
        lane = lax.broadcasted_iota(jnp.int32, ab.shape, 1)
        gb_ref[0] = jnp.where(lane < GDN_HEADS, g, _sigmoid(ab))

    h_qkv = proj(0, COL_GATE)
    h_mid = proj(COL_GATE, COL_KB)
    gdn_qkv(h_qkv)
    h_last = proj(COL_KB, D_IN_PAD)
    gate_and_q(h_mid)
    k_v_gates(h_last)


def _inproj(x, w_in_r, conv_w, cbuf8, alog, dtb, cos_t, sin_t, tm):
    B, T, D = x.shape
    nt = T // tm
    tok = lambda width, dt: jax.ShapeDtypeStruct((B, T, width), dt)
    tok_spec = lambda width: pl.BlockSpec((1, tm, width), lambda b, t: (b, t, 0))
    full = lambda a: pl.BlockSpec(a.shape, lambda b, t: (0,) * a.ndim)
    out_shape = (tok(GDN_QK, BF16), tok(GDN_QK, BF16), tok(GDN_V, BF16), tok(GDN_V, BF16),
                 tok(LANES, F32), tok(DIFF_QK, BF16), tok(DIFF_QK, BF16), tok(DIFF_V, BF16),
                 jax.ShapeDtypeStruct((B, T * DIFF_HEADS, 2 * DIFF_DH), F32),
                 jax.ShapeDtypeStruct((B, T * DIFF_HEADS, DIFF_DV), F32),
                 jax.ShapeDtypeStruct((B, SUBLANES, GDN_QKV), F32))
    out_specs = (tok_spec(GDN_QK), tok_spec(GDN_QK), tok_spec(GDN_V), tok_spec(GDN_V),
                 tok_spec(LANES), tok_spec(DIFF_QK), tok_spec(DIFF_QK), tok_spec(DIFF_V),
                 pl.BlockSpec((1, tm * DIFF_HEADS, 2 * DIFF_DH), lambda b, t: (b, t, 0)),
                 pl.BlockSpec((1, tm * DIFF_HEADS, DIFF_DV), lambda b, t: (b, t, 0)),
                 pl.BlockSpec((1, SUBLANES, GDN_QKV), lambda b, t: (b, 0, 0)))
    in_specs = [tok_spec(D), full(w_in_r), full(conv_w),
                pl.BlockSpec((1, SUBLANES, GDN_QKV), lambda b, t: (b, 0, 0)),
                full(alog), full(dtb),
                pl.BlockSpec((tm, LANES), lambda b, t: (t, 0)),
                pl.BlockSpec((tm, LANES), lambda b, t: (t, 0))]
    return pl.pallas_call(
        _inproj_kernel,
        grid=(B, nt),
        in_specs=in_specs,
        out_specs=out_specs,
        out_shape=out_shape,
        scratch_shapes=[pltpu.VMEM((SUBLANES + tm, GDN_QKV), F32)],
        compiler_params=pltpu.CompilerParams(
            dimension_semantics=("arbitrary", "arbitrary"),
            vmem_limit_bytes=VMEM_LIMIT_BYTES),
        name="inproj",
    )(x, w_in_r, conv_w, cbuf8, alog, dtb, cos_t, sin_t)


def _split_hi_lo(x):
    hi = x.astype(BF16)
    return hi, (x - hi.astype(F32)).astype(BF16)


def _blockdiag_rows(x, width):
    n = x.shape[1] // width
    blk = lax.broadcasted_iota(jnp.int32, x.shape, 1) // width
    zero = jnp.zeros_like(x)
    return jnp.concatenate([jnp.where(blk == j, x, zero) for j in range(n)], axis=0)


def _packed_dot_split(l2, r2, width):
    l_hi, l_lo = _split_hi_lo(l2)
    r_hi, r_lo = _split_hi_lo(r2)
    bd_hi = _blockdiag_rows(r_hi, width)
    bd_lo = _blockdiag_rows(r_lo, width)
    lhs = jnp.concatenate([l_hi, l_lo, l_hi, l_lo], axis=1)
    rhs = jnp.concatenate([bd_hi, bd_hi, bd_lo, bd_lo], axis=0)
    return _dot(lhs, rhs)


def _per_head(cols, lane_head):
    out = cols[0]
    for j in range(1, len(cols)):
        out = jnp.where(lane_head >= j, cols[j], out)
    return out


def _gdn_kernel(q_ref, k_ref, v_ref, gate_ref, gb_ref, s0_ref, nw_ref,
                o_ref, sout_ref, s_ref, *, chunk, n_chunks, set_chunks):
    t = pl.program_id(1)

    @pl.when(t == 0)
    def _():
        s_ref[...] = s0_ref[0]

    C = chunk
    G = LANES // C
    W = G * GDN_DK
    n_groups = GDN_HEADS // G
    row = lax.broadcasted_iota(jnp.int32, (C, LANES), 0)
    lane = lax.broadcasted_iota(jnp.int32, (C, LANES), 1)
    lane_in = lane % C
    lane_head_p = lane // C
    lane_head_w = lax.broadcasted_iota(jnp.int32, (C, W), 1) // GDN_DK
    incl = row >= lane_in
    strict = row > lane_in
    eye_b = row == lane_in
    eye = eye_b.astype(F32)
    r1 = lax.broadcasted_iota(jnp.int32, (C, C), 0)
    c1 = lax.broadcasted_iota(jnp.int32, (C, C), 1)
    tri = (r1 >= c1).astype(BF16)
    tri3 = jnp.concatenate([tri, tri, tri], axis=1)
    scale = GDN_DK ** -0.5
    nw = nw_ref[...]
    n_sq = max(int(math.ceil(math.log2(C))) - 1, 0)

    head_slices = [slice(j * GDN_DK, (j + 1) * GDN_DK) for j in range(G)]

    def local_part(chunks, out):
        items = []
        for c in chunks:
            rows = slice(c * C, (c + 1) * C)
            gbc = gb_ref[0, rows, :]
            g_hi = gbc.astype(BF16)
            g_rest = gbc - g_hi.astype(F32)
            g_mid = g_rest.astype(BF16)
            g_lo = (g_rest - g_mid.astype(F32)).astype(BF16)
            gc_all = _dot(tri3, jnp.concatenate([g_hi, g_mid, g_lo], axis=0))
            for gi in range(n_groups):
                heads = [gi * G + j for j in range(G)]
                cols = slice(gi * W, (gi + 1) * W)
                gcols = [gc_all[:, h:h + 1] for h in heads]
                bcols = [gbc[:, GDN_HEADS + h:GDN_HEADS + h + 1] for h in heads]
                gcol_p = _per_head(gcols, lane_head_p)
                grow_p = jnp.sum(jnp.where(eye_b, gcol_p, 0.0), axis=0, keepdims=True)
                decay = jnp.exp(jnp.where(incl, gcol_p - grow_p, -jnp.inf))
                b_w = _per_head(bcols, lane_head_w)
                eg_w = _per_head([jnp.exp(g) for g in gcols], lane_head_w)
                q = q_ref[0, rows, cols].astype(F32) * scale
                k = k_ref[0, rows, cols].astype(F32)
                v = v_ref[0, rows, cols].astype(F32)
                kb = k * b_w
                items.append(dict(c=c, q=q, k=k, kb=kb, vb=v * b_w, kbe=kb * eg_w, qe=q * eg_w,
                                  decay=decay, gcols=gcols))
        yield
        for it in items:
            k_bd = _blockdiag_rows(it["k"].astype(BF16), GDN_DK)
            aq = _nt_dot(jnp.concatenate([it["kb"].astype(BF16), it["q"].astype(BF16)], axis=0),
                         k_bd)
            it["qk"] = aq[C:] * it["decay"]
            it["p"] = -jnp.where(strict, aq[:C] * it["decay"], 0.0)
            it["t"] = eye + it["p"]
        yield
        if n_sq > 0:
            for it in items:
                it["p"] = _packed_dot_split(it["p"], it["p"], C)
            yield
            for _ in range(n_sq - 1):
                for it in items:
                    both = _packed_dot_split(jnp.concatenate([it["t"], it["p"]], axis=0),
                                             it["p"], C)
                    it["t"] = it["t"] + both[:C]
                    it["p"] = both[C:]
                yield
            for it in items:
                it["t"] = it["t"] + _packed_dot_split(it["t"], it["p"], C)
            yield
        for it in items:
            rhs = jnp.concatenate([_blockdiag_rows(it["vb"].astype(BF16), GDN_DV),
                                   _blockdiag_rows(it["kbe"].astype(BF16), GDN_DK)], axis=1)
            uw = _dot(it["t"].astype(BF16), rhs)
            out.setdefault(it["c"], []).append(
                (uw[:, :W], uw[:, W:], it["qk"], it["qe"], it["k"], it["gcols"]))

    state = [s_ref[h] for h in range(GDN_HEADS)]

    def recurrence(chunks, local):
        for c in chunks:
            rows = slice(c * C, (c + 1) * C)
            groups = local[c]
            r = [[_dot(jnp.concatenate([w[:, hs], qe[:, hs]], axis=0).astype(BF16),
                       state[gi * G + j].astype(BF16))
                  for j, hs in enumerate(head_slices)]
                 for gi, (u, w, qk, qe, k, gcols) in enumerate(groups)]
            yield
            v_new_bf = [[(u[:, hs] - r[gi][j][:C]).astype(BF16)
                         for j, hs in enumerate(head_slices)]
                        for gi, (u, w, qk, qe, k, gcols) in enumerate(groups)]
            o = [jnp.concatenate([r[gi][j][C:] for j in range(G)], axis=1)
                 + _dot(qk.astype(BF16),
                        _blockdiag_rows(jnp.concatenate(v_new_bf[gi], axis=1), GDN_DV))
                 for gi, (u, w, qk, qe, k, gcols) in enumerate(groups)]
            for gi, (u, w, qk, qe, k, gcols) in enumerate(groups):
                for j, hs in enumerate(head_slices):
                    h = gi * G + j
                    g_last = gcols[j][C - 1:C, :]
                    k_dec = (k[:, hs] * jnp.exp(g_last - gcols[j])).astype(BF16)
                    state[h] = state[h] * jnp.exp(g_last) + _tn_dot(k_dec, v_new_bf[gi][j])
            yield
            for gi in range(n_groups):
                for j, hs in enumerate(head_slices):
                    h = gi * G + j
                    oh = o[gi][:, hs]
                    on = oh * lax.rsqrt(jnp.mean(oh * oh, axis=-1, keepdims=True) + RMS_EPS) * nw
                    ocols = slice(h * GDN_DV, (h + 1) * GDN_DV)
                    o_ref[0, rows, ocols] = (on * gate_ref[0, rows, ocols].astype(F32)
                                             ).astype(BF16)
            yield

    def alternate(*gens):
        gens = list(gens)
        done = object()
        while gens:
            for g in list(gens):
                if next(g, done) is done:
                    gens.remove(g)

    sets = [list(range(c0, min(c0 + set_chunks, n_chunks))) for c0 in range(0, n_chunks, set_chunks)]
    local = {}
    alternate(local_part(sets[0], local))
    for prev, cur in zip(sets[:-1], sets[1:]):
        alternate(local_part(cur, local), recurrence(prev, local))
    alternate(recurrence(sets[-1], local))

    for h in range(GDN_HEADS):
        s_ref[h] = state[h]
        sout_ref[0, h] = state[h]


def _gdn(qg, kg, vg, gate, gb, s0, norm_w, chunk, n_chunks):
    B, T, _ = qg.shape
    tb = chunk * n_chunks
    nt = T // tb
    tok_spec = lambda width: pl.BlockSpec((1, tb, width), lambda b, t: (b, t, 0))
    st_spec = pl.BlockSpec((1, GDN_HEADS, GDN_DK, GDN_DV), lambda b, t: (b, 0, 0, 0))
    return pl.pallas_call(
        functools.partial(_gdn_kernel, chunk=chunk, n_chunks=n_chunks,
                          set_chunks=GDN_SET_CHUNKS),
        grid=(B, nt),
        in_specs=[tok_spec(GDN_QK), tok_spec(GDN_QK), tok_spec(GDN_V), tok_spec(GDN_V),
                  tok_spec(LANES), st_spec, pl.BlockSpec((1, GDN_DV), lambda b, t: (0, 0))],
        out_specs=(tok_spec(GDN_V), st_spec),
        out_shape=(jax.ShapeDtypeStruct((B, T, GDN_V), BF16),
                   jax.ShapeDtypeStruct((B, GDN_HEADS, GDN_DK, GDN_DV), F32)),
        scratch_shapes=[pltpu.VMEM((GDN_HEADS, GDN_DK, GDN_DV), F32)],
        compiler_params=pltpu.CompilerParams(
            dimension_semantics=("arbitrary", "arbitrary"),
            vmem_limit_bytes=VMEM_LIMIT_BYTES),
        name="gdn",
    )(qg, kg, vg, gate, gb, s0, norm_w)


def _lane_tile(x, width):
    if width % LANES == 0:
        return jnp.concatenate([x] * (width // LANES), axis=1)
    return x[:, :1]


def _attn_kernel(*refs, bq, bk, bk_wide, cache_bk, unroll, q_pos0, k_pos0, lam_init):
    if cache_bk:
        q_ref, k_ref, v_ref, kc_ref, vc_ref, lam_ref, w_ref, o_ref, m_ref, acc_ref = refs
    else:
        q_ref, k_ref, v_ref, lam_ref, w_ref, o_ref, m_ref, acc_ref = refs
    head = pl.program_id(1)
    qi = pl.program_id(2)
    tk = k_ref.shape[1]

    q = q_ref[0]
    lane = lax.broadcasted_iota(jnp.int32, q.shape, 1)
    zero = jnp.zeros_like(q)
    q_maps = [jnp.where(lane < DIFF_DH, q, zero), jnp.where(lane >= DIFF_DH, q, zero)]
    small = bq < LANES
    parts = [(jnp.concatenate(q_maps, axis=0), (0, 1))] if small else [(q_maps[0], (0,)),
                                                                         (q_maps[1], (1,))]

    q_first = q_pos0 + qi * bq
    q_last = q_first + (bq - 1)
    vis_first = (q_first // CHUNK + 1) * CHUNK - k_pos0
    vis_last = (q_last // CHUNK + 1) * CHUNK - k_pos0
    per_wide = bk_wide // bk
    n_wide = jnp.clip(vis_first // bk_wide, 0, tk // bk_wide)
    n_full = jnp.clip(vis_first // bk, 0, tk // bk)
    n_all = jnp.clip((vis_last + bk - 1) // bk, 0, tk // bk)

    m_ref[...] = jnp.full(m_ref.shape, NEG_INF, F32)
    acc_ref[...] = jnp.zeros(acc_ref.shape, F32)

    def load_new(start, width, ref):
        return ref[0, pl.ds(pl.multiple_of(start, bk), width), :].astype(BF16)

    def load_cache(start, width, ref):
        rows = pl.ds(start * DIFF_HEADS + head, width, stride=DIFF_HEADS)
        return ref[0, rows, :].astype(BF16)

    def steps(i, carry, masked, n, base, width, load, kr, vr):
        starts = [base + (i * n + j) * width for j in range(n)]
        scores = [[_nt_dot(qp, load(st, width, kr)) for qp, _ in parts] for st in starts]
        ones = jnp.ones((width, LANES), BF16)
        m = [jnp.concatenate([m_ref[mp] for mp in maps], axis=0) for _, maps in parts]
        acc = [jnp.concatenate([acc_ref[mp] for mp in maps], axis=0) for _, maps in parts]
        for st, s_parts in zip(starts, scores):
            v_blk = load(st, width, vr)
            if not small:
                v_blk = jnp.concatenate([v_blk, ones], axis=1)
            for pi, s in enumerate(s_parts):
                if masked:
                    kpos = k_pos0 + st + lax.broadcasted_iota(jnp.int32, s.shape, 1)
                    qpos = q_first + lax.broadcasted_iota(jnp.int32, s.shape, 0) % bq
                    s = jnp.where(kpos // CHUNK <= qpos // CHUNK, s, NEG_INF)
                m_new = jnp.maximum(m[pi], jnp.max(s, axis=-1, keepdims=True))
                p = jnp.exp2(s - _lane_tile(m_new, width))
                alpha = jnp.exp2(m[pi] - m_new)
                pv = _dot(p.astype(BF16), v_blk)
                if small:
                    row_sum = jnp.sum(p, axis=-1, keepdims=True)
                    pv = jnp.concatenate([pv, jnp.broadcast_to(row_sum, pv.shape)], axis=1)
                acc[pi] = jnp.concatenate([alpha, alpha], axis=1) * acc[pi] + pv
                m[pi] = m_new
        for pi, (_, maps) in enumerate(parts):
            for r, mp in enumerate(maps):
                m_ref[mp] = m[pi][r * bq:(r + 1) * bq]
                acc_ref[mp] = acc[pi][r * bq:(r + 1) * bq]
        return carry

    def unmasked_run(count, base, width, load, kr, vr, max_unroll):
        done = 0
        u = max_unroll
        while u >= 1:
            trips = (count - done) // u
            lax.fori_loop(0, trips,
                          functools.partial(steps, masked=False, n=u, base=base + done * width,
                                            width=width, load=load, kr=kr, vr=vr), 0)
            done = done + trips * u
            u //= 2

    if cache_bk:
        unmasked_run(kc_ref.shape[1] // (cache_bk * DIFF_HEADS), 0, cache_bk, load_cache, kc_ref,
                     vc_ref, unroll)
    unmasked_run(n_wide, 0, bk_wide, load_new, k_ref, v_ref, unroll)
    unmasked_run(n_full - n_wide * per_wide, n_wide * bk_wide, bk, load_new, k_ref, v_ref, 1)
    lax.fori_loop(0, n_all - n_full,
                  functools.partial(steps, masked=True, n=1, base=n_full * bk, width=bk,
                                    load=load_new, kr=k_ref, vr=v_ref), 0)

    lv = lam_ref[...]
    lam = (jnp.exp(jnp.sum(lv[0:1] * lv[1:2], axis=-1, keepdims=True))
           - jnp.exp(jnp.sum(lv[2:3] * lv[3:4], axis=-1, keepdims=True)) + lam_init)
    o = [acc_ref[mp, :, :DIFF_DV] / acc_ref[mp, :, DIFF_DV:] for mp in range(2)]
    out = o[0] - lam * o[1]
    out = out * lax.rsqrt(jnp.mean(out * out, axis=-1, keepdims=True) + RMS_EPS) * w_ref[...]
    o_ref[0] = (out * (1.0 - lam_init)).astype(BF16)


def _attn(qd, k_new, v_new, k_cache, v_cache, lam_p, subln_w, bq, bk, bk_wide, cache_bk, unroll,
          q_pos0, lam_init):
    B, Tq, _ = qd.shape
    Tk = k_new.shape[1]
    nq = Tq // bq
    kv_spec = pl.BlockSpec((1, Tk, LANES), lambda b, h, i: (b, 0, h))
    in_specs = [pl.BlockSpec((1, bq, LANES), lambda b, h, i: (b, i, h)), kv_spec, kv_spec]
    args = [qd, k_new, v_new]
    k_pos0 = 0
    if k_cache is not None:
        cache_spec = pl.BlockSpec((1,) + k_cache.shape[1:], lambda b, h, i: (b, 0, 0))
        in_specs += [cache_spec, cache_spec]
        args += [k_cache, v_cache]
        k_pos0 = k_cache.shape[1] // DIFF_HEADS
    in_specs += [pl.BlockSpec(lam_p.shape, lambda b, h, i: (0, 0)),
                 pl.BlockSpec(subln_w.shape, lambda b, h, i: (0, 0))]
    args += [lam_p, subln_w]
    return pl.pallas_call(
        functools.partial(_attn_kernel, bq=bq, bk=bk, bk_wide=bk_wide,
                          cache_bk=cache_bk if k_cache is not None else 0, unroll=unroll,
                          q_pos0=q_pos0, k_pos0=k_pos0, lam_init=lam_init),
        grid=(B, DIFF_HEADS, nq),
        in_specs=in_specs,
        out_specs=pl.BlockSpec((1, bq, LANES), lambda b, h, i: (b, i, h)),
        out_shape=jax.ShapeDtypeStruct((B, Tq, DIFF_V), BF16),
        scratch_shapes=[pltpu.VMEM((2, bq, LANES), F32),
                        pltpu.VMEM((2, bq, DIFF_DV + LANES), F32)],
        compiler_params=pltpu.CompilerParams(
            dimension_semantics=("arbitrary", "arbitrary", "arbitrary"),
            vmem_limit_bytes=VMEM_LIMIT_BYTES),
        name="diffattn",
    )(*args)


def _ffn_kernel(x_ref, oa_ref, ob_ref, wo_ref, g1_ref, b1_ref, wup_ref, cw_ref, cb_ref, cbuf_ref,
                wdn_ref, g2_ref, b2_ref, y_ref, cstate_ref, prev_ref, *, alpha, d_ff,
                chunk_cols):
    t = pl.program_id(1)

    @pl.when(t == 0)
    def _():
        prev_ref[...] = cbuf_ref[0]

    tm = x_ref.shape[1]
    mix = _dot(oa_ref[0], wo_ref[:GDN_V, :]) + _dot(ob_ref[0], wo_ref[GDN_V:, :])
    x1 = _layer_norm(alpha * x_ref[0] + mix, g1_ref[...], b1_ref[...])
    x1b = x1.astype(BF16)

    def conv_up(col, width):
        cols = slice(col, col + width)
        up = _dot(x1b, wup_ref[:, cols])
        prev = prev_ref[:, cols]
        cw = cw_ref[:, cols]
        u = up * cw[FFN_CONV - 1:FFN_CONV] + cb_ref[:, cols]
        for j in range(FFN_CONV - 1):
            u = u + _shifted_rows(prev, up, FFN_CONV - 1 - j) * cw[j:j + 1]
        tail = up[tm - SUBLANES:]
        prev_ref[:, cols] = tail
        cstate_ref[0, :, cols] = tail
        return u

    chunks = [(c0, min(chunk_cols, d_ff - c0)) for c0 in range(0, d_ff, chunk_cols)]
    gates = [(conv_up(c0, cs), conv_up(d_ff + c0, cs)) for c0, cs in chunks[:1]]
    y = None
    for i, (c0, cs) in enumerate(chunks):
        if i + 1 < len(chunks):
            n0, ns = chunks[i + 1]
            gates.append((conv_up(n0, ns), conv_up(d_ff + n0, ns)))
        ua, ub = gates[i]
        part = _dot((_silu(ua) * ub).astype(BF16), wdn_ref[c0:c0 + cs, :])
        y = part if y is None else y + part
    y_ref[0] = _layer_norm(alpha * x1 + y, g2_ref[...], b2_ref[...])


def _ffn(x, oa, ob, w_o, g1, b1, w_up, cw, cb, cbuf8, w_dn, g2, b2, tm, alpha):
    B, T, D = x.shape
    nt = T // tm
    d_ff = w_dn.shape[0]
    tok_spec = lambda width: pl.BlockSpec((1, tm, width), lambda b, t: (b, t, 0))
    full = lambda a: pl.BlockSpec(a.shape, lambda b, t: (0,) * a.ndim)
    once = lambda a: pl.BlockSpec(a.shape, lambda b, t: (0,) * a.ndim,
                                  pipeline_mode=pl.Buffered(1))
    st_spec = pl.BlockSpec((1, SUBLANES, 2 * d_ff), lambda b, t: (b, 0, 0))
    return pl.pallas_call(
        functools.partial(_ffn_kernel, alpha=alpha, d_ff=d_ff, chunk_cols=FFN_CHUNK_COLS),
        grid=(B, nt),
        in_specs=[tok_spec(D), tok_spec(GDN_V), tok_spec(DIFF_V), once(w_o), full(g1), full(b1),
                  once(w_up), full(cw), full(cb), st_spec, once(w_dn), full(g2), full(b2)],
        out_specs=(tok_spec(D), st_spec),
        out_shape=(jax.ShapeDtypeStruct((B, T, D), F32),
                   jax.ShapeDtypeStruct((B, SUBLANES, 2 * d_ff), F32)),
        scratch_shapes=[pltpu.VMEM((SUBLANES, 2 * d_ff), F32)],
        compiler_params=pltpu.CompilerParams(
            dimension_semantics=("arbitrary", "arbitrary"),
            vmem_limit_bytes=VMEM_LIMIT_BYTES),
        name="ffn",
    )(x, oa, ob, w_o, g1, b1, w_up, cw, cb, cbuf8, w_dn, g2, b2)


def _pad_rows_front(buf, rows):
    B, r, C = buf.shape
    return jnp.concatenate([jnp.zeros((B, rows - r, C), buf.dtype), buf], axis=1)


def _lane_row(vec):
    n = vec.shape[0]
    pad = (-n) % LANES
    return jnp.pad(vec.astype(F32), (0, pad)).reshape(1, n + pad)


def _rope_tables(pos0, T):
    half = DIFF_DH // 2
    inv = ROPE_THETA ** (-jnp.arange(half, dtype=F32) * (2.0 / DIFF_DH))
    pos = pos0 + jnp.arange(T, dtype=jnp.int32)
    ang = pos.astype(F32)[:, None] * inv[None, :]
    cos = jnp.cos(ang)
    sin = jnp.sin(ang)
    reps = LANES // DIFF_DH
    cos_t = jnp.tile(jnp.concatenate([cos, cos], axis=-1), (1, reps))
    sin_t = jnp.tile(jnp.concatenate([-sin, sin], axis=-1), (1, reps))
    return cos_t, sin_t


def _largest_divisor(n, cap, mult):
    best = None
    for d in range(mult, min(n, cap) + 1, mult):
        if n % d == 0:
            best = d
    return best if best is not None else n


def _layer(x, pos0, k_past, v_past, s0, conv_qkv_buf, conv_ffn_buf, p, lam_init, alpha):
    B, T, _ = x.shape
    tm = _largest_divisor(T, 512, SUBLANES)
    cos_t, sin_t = _rope_tables(pos0, T)
    (qg, kg, vg, gate, gb, qd, kd, vd, k_new, v_new, cq8) = _inproj(
        x, p["w_in"], p["gdn_conv_w"], _pad_rows_front(conv_qkv_buf, SUBLANES),
        p["alog"], p["dtb"], cos_t, sin_t, tm)

    chunk = CHUNK if T % CHUNK == 0 else T
    n_chunks = _largest_divisor(T // chunk, GDN_STEP_CHUNKS, 1)
    oa, s_new = _gdn(qg, kg, vg, gate, gb, s0, p["gdn_norm_w"], chunk, n_chunks)

    if k_past is None:
        k_cache = v_cache = None
        cache_bk = 0
    else:
        k_cache = k_past.reshape(B, -1, 2 * DIFF_DH)
        v_cache = v_past.reshape(B, -1, DIFF_DV)
        cache_bk = _largest_divisor(k_past.shape[1], 1024, 16)
    bq = _largest_divisor(T, 512, 16)
    bk = _largest_divisor(T, 512, 16)
    bk_wide = 2 * bk if T % (2 * bk) == 0 else bk
    ob = _attn(qd, kd, vd, k_cache, v_cache, p["diff_lambda"], p["diff_subln_w"], bq, bk,
               bk_wide, cache_bk, 4, pos0, lam_init)

    tm_f = _largest_divisor(T, 512, SUBLANES)
    y, cf8 = _ffn(x, oa, ob, p["w_o"], p["ln1_g"], p["ln1_b"], p["w_up"], p["ffn_conv_w"],
                  p["ffn_conv_b"], _pad_rows_front(conv_ffn_buf, SUBLANES), p["w_down"],
                  p["ln2_g"], p["ln2_b"], tm_f, alpha)

    new_k = k_new.reshape(B, T, DIFF_HEADS, 2 * DIFF_DH)
    new_v = v_new.reshape(B, T, DIFF_HEADS, DIFF_DV)
    return (y, new_k, new_v, s_new, cq8[:, SUBLANES - (GDN_CONV - 1):],
            cf8[:, SUBLANES - (FFN_CONV - 1):])


def _prep_params(l, w_in, gdn_conv_w, gdn_a_log, gdn_dt_bias, gdn_norm_w, diff_lambda,
                 diff_subln_w, w_o, ln1_g, ln1_b, w_up, ffn_conv_w, ffn_conv_b, w_down,
                 ln2_g, ln2_b):
    o1 = GDN_QKV
    o2 = o1 + GDN_V
    o3 = o2 + GDN_HEADS
    o4 = o3 + GDN_HEADS
    w = w_in[l]
    d_model = w.shape[0]
    w_r = jnp.concatenate(
        [w[:, :o2], w[:, o4:], w[:, o2:o4],
         jnp.zeros((d_model, LANES - 2 * GDN_HEADS), w.dtype)], axis=1).astype(BF16)
    row = lambda v: v.astype(F32).reshape(1, -1)
    return {
        "w_in": w_r, "gdn_conv_w": gdn_conv_w[l].astype(F32),
        "alog": _lane_row(gdn_a_log[l]), "dtb": _lane_row(gdn_dt_bias[l]),
        "gdn_norm_w": row(gdn_norm_w[l]), "diff_lambda": diff_lambda[l].astype(F32),
        "diff_subln_w": row(diff_subln_w[l]), "w_o": w_o[l].astype(BF16),
        "ln1_g": row(ln1_g[l]), "ln1_b": row(ln1_b[l]), "w_up": w_up[l].astype(BF16),
        "ffn_conv_w": ffn_conv_w[l].astype(F32), "ffn_conv_b": row(ffn_conv_b[l]),
        "w_down": w_down[l].astype(BF16), "ln2_g": row(ln2_g[l]), "ln2_b": row(ln2_b[l]),
    }


def kernel(x_prompt, x_sample, cache_k, cache_v, state_gdn, state_conv_qkv, state_conv_ffn, w_in, gdn_conv_w, gdn_a_log, gdn_dt_bias, gdn_norm_w, diff_lambda, diff_subln_w, w_o, ln1_g, ln1_b, w_up, ffn_conv_w, ffn_conv_b, w_down, ln2_g, ln2_b):
    depth = w_in.shape[0]
    Bp = x_prompt.shape[0]
    past = cache_k.shape[2]
    alpha = (2 * depth) ** 0.25
    xp, xs = x_prompt, x_sample
    outs_p, outs_s = [], []
    for l in range(depth):
        p = _prep_params(l, w_in, gdn_conv_w, gdn_a_log, gdn_dt_bias, gdn_norm_w, diff_lambda,
                         diff_subln_w, w_o, ln1_g, ln1_b, w_up, ffn_conv_w, ffn_conv_b, w_down,
                         ln2_g, ln2_b)
        lam_init = 0.8 - 0.6 * math.exp(-0.3 * l)
        s0_p = jnp.zeros((Bp, GDN_HEADS, GDN_DK, GDN_DV), F32)
        cq0_p = jnp.zeros((Bp, GDN_CONV - 1, GDN_QKV), xp.dtype)
        cf0_p = jnp.zeros((Bp, FFN_CONV - 1, state_conv_ffn.shape[-1]), xp.dtype)
        xp, kp, vp, sp, cqp, cfp = _layer(xp, 0, None, None, s0_p, cq0_p, cf0_p, p, lam_init, alpha)
        xs, ksn, vsn, ssn, cqs, cfs = _layer(xs, past, cache_k[l], cache_v[l], state_gdn[l],
                                             state_conv_qkv[l], state_conv_ffn[l], p, lam_init,
                                             alpha)
        outs_p.append((kp, vp, sp, cqp, cfp))
        outs_s.append((ksn, vsn, ssn, cqs, cfs))

    def stk(outs, i):
        return jnp.stack([o[i] for o in outs])

    return (xp, xs,
            stk(outs_p, 0), stk(outs_p, 1), stk(outs_p, 2), stk(outs_p, 3), stk(outs_p, 4),
            stk(outs_s, 0), stk(outs_s, 1), stk(outs_s, 2), stk(outs_s, 3), stk(outs_s, 4))
```

```python
import functools
import math

import jax
import jax.numpy as jnp
from jax import lax
from jax.experimental import pallas as pl
from jax.experimental.pallas import tpu as pltpu

F32 = jnp.float32
BF16 = jnp.bfloat16

LANES = 128
SUBLANES = 8
VMEM_LIMIT_BYTES = 56 * 1024 * 1024
FFN_ROW_PARTS = 2
INPROJ_ROW_PARTS = 4
GDN_STEP_CHUNKS = 16
GDN_SET_CHUNKS = 4

CHUNK = 64
GDN_HEADS = 4
GDN_DK = 128
GDN_DV = 128
GDN_CONV = 4
DIFF_HEADS = 4
DIFF_DH = 64
DIFF_DV = 2 * DIFF_DH
ROPE_THETA = 10000.0
FFN_CONV = 3
LN_EPS = 1e-5
RMS_EPS = 1e-6
L2_EPS = 1e-6
NEG_INF = -1e30

GDN_QK = GDN_HEADS * GDN_DK
GDN_V = GDN_HEADS * GDN_DV
GDN_QKV = 2 * GDN_QK + GDN_V
DIFF_QK = DIFF_HEADS * 2 * DIFF_DH
DIFF_V = DIFF_HEADS * DIFF_DV

COL_GATE = GDN_QKV
COL_QB = COL_GATE + GDN_V
COL_KB = COL_QB + DIFF_QK
COL_VB = COL_KB + DIFF_QK
COL_AB = COL_VB + DIFF_V
D_IN_PAD = COL_AB + LANES


def _nt_dot(a, b):
    return lax.dot_general(a, b, (((1,), (1,)), ((), ())), preferred_element_type=F32)


def _tn_dot(a, b):
    return lax.dot_general(a, b, (((0,), (0,)), ((), ())), preferred_element_type=F32)


def _dot(a, b):
    return jnp.dot(a, b, preferred_element_type=F32)


def _sigmoid(x):
    return 1.0 / (1.0 + jnp.exp(-x))


def _silu(x):
    return x * _sigmoid(x)


def _softplus(x):
    return jnp.maximum(x, 0.0) + jnp.log(1.0 + jnp.exp(-jnp.abs(x)))


def _layer_norm(x, g, b):
    mu = jnp.mean(x, axis=-1, keepdims=True)
    xc = x - mu
    var = jnp.mean(xc * xc, axis=-1, keepdims=True)
    return xc * lax.rsqrt(var + LN_EPS) * g + b


def _shifted_rows(prev, cur, back):
    if back == 0:
        return cur
    ext = jnp.concatenate([prev, cur], axis=0)
    n = cur.shape[0]
    return ext[SUBLANES - back:SUBLANES - back + n]


def _inproj_kernel(x_ref, w_ref, convw_ref, cbuf_ref, alog_ref, dtb_ref, cos_ref, sin_ref,
                   qg_ref, kg_ref, vg_ref, gate_ref, gb_ref, qd_ref, kd_ref, vd_ref,
                   kout_ref, vout_ref, cstate_ref, prev_ref, *, row_parts):
    t = pl.program_id(1)

    @pl.when(t == 0)
    def _():
        prev_ref[:SUBLANES, :] = cbuf_ref[0]

    tm = x_ref.shape[1]

    def rows_part(r0, n):
        rows = slice(r0, r0 + n)
        x = x_ref[0, rows].astype(BF16)
        proj = lambda lo, hi: _dot(x, w_ref[:, lo:hi])
        cos = cos_ref[rows]
        sin = sin_ref[rows]

        def rope(z):
            first_half = (lax.broadcasted_iota(jnp.int32, z.shape, 1) % DIFF_DH) < (DIFF_DH // 2)
            swapped = jnp.where(first_half, pltpu.roll(z, LANES - DIFF_DH // 2, 1),
                                pltpu.roll(z, DIFF_DH // 2, 1))
            return z * cos + swapped * sin

        head_rows = lambda hd: pl.ds(hd + r0 * DIFF_HEADS, n, stride=DIFF_HEADS)

        qkv = proj(0, COL_GATE)
        yield
        hm = proj(COL_GATE, COL_KB)
        yield
        cw = convw_ref[...]
        prev_ref[SUBLANES + r0:SUBLANES + r0 + n, :] = qkv
        y = qkv * cw[GDN_CONV - 1:GDN_CONV]
        for j in range(GDN_CONV - 1):
            back = GDN_CONV - 1 - j
            y = y + prev_ref[pl.ds(SUBLANES + r0 - back, n), :] * cw[j:j + 1]
        if r0 + n == tm:
            tail = qkv[n - SUBLANES:]
            prev_ref[:SUBLANES, :] = tail
            cstate_ref[0] = tail
        y = _silu(y)
        for hd in range(GDN_HEADS):
            for base, ref in ((0, qg_ref), (GDN_QK, kg_ref)):
                z = y[:, base + hd * GDN_DK: base + (hd + 1) * GDN_DK]
                z = z * lax.rsqrt(jnp.sum(z * z, axis=-1, keepdims=True) + L2_EPS)
                ref[0, rows, hd * GDN_DK:(hd + 1) * GDN_DK] = z.astype(BF16)
        vg_ref[0, rows] = y[:, 2 * GDN_QK:].astype(BF16)
        yield
        hl = proj(COL_KB, D_IN_PAD)
        yield
        gate_ref[0, rows] = _silu(hm[:, :GDN_V]).astype(BF16)
        scale = DIFF_DH ** -0.5 * math.log2(math.e)
        for hd in range(DIFF_HEADS):
            sl = slice(hd * LANES, (hd + 1) * LANES)
            qd_ref[0, rows, sl] = (rope(hm[:, GDN_V + hd * LANES:GDN_V + (hd + 1) * LANES])
                                   * scale).astype(BF16)
        yield
        for hd in range(DIFF_HEADS):
            sl = slice(hd * LANES, (hd + 1) * LANES)
            r = rope(hl[:, sl])
            kout_ref[0, head_rows(hd), :] = r
            kd_ref[0, rows, sl] = r.astype(BF16)
        vb = hl[:, DIFF_QK:DIFF_QK + DIFF_V]
        for hd in range(DIFF_HEADS):
            vout_ref[0, head_rows(hd), :] = vb[:, hd * DIFF_DV:(hd + 1) * DIFF_DV]
        vd_ref[0, rows] = vb.astype(BF16)
        ab = hl[:, DIFF_QK + DIFF_V:]
        g = -jnp.exp(alog_ref[...]) * _softplus(ab + dtb_ref[...])
        lane = lax.broadcasted_iota(jnp.int32, ab.shape, 1)
        gb_ref[0, rows] = jnp.where(lane < GDN_HEADS, g, _sigmoid(ab))

    n_parts = row_parts if tm % (row_parts * SUBLANES * 2) == 0 else 1
    n = tm // n_parts
    waiting = [rows_part(i * n, n) for i in range(n_parts)]
    live = []
    done = object()
    while waiting or live:
        if waiting:
            live.append(waiting.pop(0))
            next(live[-1])
        for g in list(live):
            if next(g, done) is done:
                live.remove(g)


def _inproj(x, w_in_r, conv_w, cbuf8, alog, dtb, cos_t, sin_t, tm):
    B, T, D = x.shape
    nt = T // tm
    tok = lambda width, dt: jax.ShapeDtypeStruct((B, T, width), dt)
    tok_spec = lambda width: pl.BlockSpec((1, tm, width), lambda b, t: (b, t, 0))
    full = lambda a: pl.BlockSpec(a.shape, lambda b, t: (0,) * a.ndim)
    out_shape = (tok(GDN_QK, BF16), tok(GDN_QK, BF16), tok(GDN_V, BF16), tok(GDN_V, BF16),
                 tok(LANES, F32), tok(DIFF_QK, BF16), tok(DIFF_QK, BF16), tok(DIFF_V, BF16),
                 jax.ShapeDtypeStruct((B, T * DIFF_HEADS, 2 * DIFF_DH), F32),
                 jax.ShapeDtypeStruct((B, T * DIFF_HEADS, DIFF_DV), F32),
                 jax.ShapeDtypeStruct((B, SUBLANES, GDN_QKV), F32))
    out_specs = (tok_spec(GDN_QK), tok_spec(GDN_QK), tok_spec(GDN_V), tok_spec(GDN_V),
                 tok_spec(LANES), tok_spec(DIFF_QK), tok_spec(DIFF_QK), tok_spec(DIFF_V),
                 pl.BlockSpec((1, tm * DIFF_HEADS, 2 * DIFF_DH), lambda b, t: (b, t, 0)),
                 pl.BlockSpec((1, tm * DIFF_HEADS, DIFF_DV), lambda b, t: (b, t, 0)),
                 pl.BlockSpec((1, SUBLANES, GDN_QKV), lambda b, t: (b, 0, 0)))
    in_specs = [tok_spec(D), full(w_in_r), full(conv_w),
                pl.BlockSpec((1, SUBLANES, GDN_QKV), lambda b, t: (b, 0, 0)),
                full(alog), full(dtb),
                pl.BlockSpec((tm, LANES), lambda b, t: (t, 0)),
                pl.BlockSpec((tm, LANES), lambda b, t: (t, 0))]
    return pl.pallas_call(
        functools.partial(_inproj_kernel, row_parts=INPROJ_ROW_PARTS),
        grid=(B, nt),
        in_specs=in_specs,
        out_specs=out_specs,
        out_shape=out_shape,
        scratch_shapes=[pltpu.VMEM((SUBLANES + tm, GDN_QKV), F32)],
        compiler_params=pltpu.CompilerParams(
            dimension_semantics=("arbitrary", "arbitrary"),
            vmem_limit_bytes=VMEM_LIMIT_BYTES),
        name="inproj",
    )(x, w_in_r, conv_w, cbuf8, alog, dtb, cos_t, sin_t)


def _split_hi_lo(x):
    hi = x.astype(BF16)
    return hi, (x - hi.astype(F32)).astype(BF16)


def _blockdiag_rows(x, width):
    n = x.shape[1] // width
    blk = lax.broadcasted_iota(jnp.int32, x.shape, 1) // width
    zero = jnp.zeros_like(x)
    return jnp.concatenate([jnp.where(blk == j, x, zero) for j in range(n)], axis=0)


def _packed_dot_split(l2, r2, width):
    l_hi, l_lo = _split_hi_lo(l2)
    r_hi, r_lo = _split_hi_lo(r2)
    bd_hi = _blockdiag_rows(r_hi, width)
    bd_lo = _blockdiag_rows(r_lo, width)
    lhs = jnp.concatenate([l_hi, l_lo, l_hi, l_lo], axis=1)
    rhs = jnp.concatenate([bd_hi, bd_hi, bd_lo, bd_lo], axis=0)
    return _dot(lhs, rhs)


def _per_head(cols, lane_head):
    out = cols[0]
    for j in range(1, len(cols)):
        out = jnp.where(lane_head >= j, cols[j], out)
    return out


def _gdn_kernel(q_ref, k_ref, v_ref, gate_ref, gb_ref, s0_ref, nw_ref,
                o_ref, sout_ref, s_ref, *, chunk, n_chunks, set_chunks):
    t = pl.program_id(1)

    @pl.when(t == 0)
    def _():
        s_ref[...] = s0_ref[0]

    C = chunk
    G = LANES // C
    W = G * GDN_DK
    n_groups = GDN_HEADS // G
    row = lax.broadcasted_iota(jnp.int32, (C, LANES), 0)
    lane = lax.broadcasted_iota(jnp.int32, (C, LANES), 1)
    lane_in = lane % C
    lane_head_p = lane // C
    lane_head_w = lax.broadcasted_iota(jnp.int32, (C, W), 1) // GDN_DK
    incl = row >= lane_in
    strict = row > lane_in
    eye_b = row == lane_in
    eye = eye_b.astype(F32)
    r1 = lax.broadcasted_iota(jnp.int32, (C, C), 0)
    c1 = lax.broadcasted_iota(jnp.int32, (C, C), 1)
    tri = (r1 >= c1).astype(BF16)
    tri3 = jnp.concatenate([tri, tri, tri], axis=1)
    scale = GDN_DK ** -0.5
    nw = nw_ref[...]
    n_sq = max(int(math.ceil(math.log2(C))) - 1, 0)

    head_slices = [slice(j * GDN_DK, (j + 1) * GDN_DK) for j in range(G)]

    def local_part(chunks, out):
        items = []
        for c in chunks:
            rows = slice(c * C, (c + 1) * C)
            gbc = gb_ref[0, rows, :]
            g_hi = gbc.astype(BF16)
            g_rest = gbc - g_hi.astype(F32)
            g_mid = g_rest.astype(BF16)
            g_lo = (g_rest - g_mid.astype(F32)).astype(BF16)
            gc_all = _dot(tri3, jnp.concatenate([g_hi, g_mid, g_lo], axis=0))
            for gi in range(n_groups):
                heads = [gi * G + j for j in range(G)]
                cols = slice(gi * W, (gi + 1) * W)
                gcols = [gc_all[:, h:h + 1] for h in heads]
                bcols = [gbc[:, GDN_HEADS + h:GDN_HEADS + h + 1] for h in heads]
                gcol_p = _per_head(gcols, lane_head_p)
                grow_p = jnp.sum(jnp.where(eye_b, gcol_p, 0.0), axis=0, keepdims=True)
                decay = jnp.exp(jnp.where(incl, gcol_p - grow_p, -jnp.inf))
                b_w = _per_head(bcols, lane_head_w)
                eg_w = _per_head([jnp.exp(g) for g in gcols], lane_head_w)
                q = q_ref[0, rows, cols].astype(F32) * scale
                k = k_ref[0, rows, cols].astype(F32)
                v = v_ref[0, rows, cols].astype(F32)
                kb = k * b_w
                items.append(dict(c=c, q=q, k=k, kb=kb, vb=v * b_w, kbe=kb * eg_w, qe=q * eg_w,
                                  decay=decay, gcols=gcols))
        yield
        for it in items:
            k_bd = _blockdiag_rows(it["k"].astype(BF16), GDN_DK)
            aq = _nt_dot(jnp.concatenate([it["kb"].astype(BF16), it["q"].astype(BF16)], axis=0),
                         k_bd)
            it["qk"] = aq[C:] * it["decay"]
            it["p"] = -jnp.where(strict, aq[:C] * it["decay"], 0.0)
            it["t"] = eye + it["p"]
        yield
        if n_sq > 0:
            for it in items:
                it["p"] = _packed_dot_split(it["p"], it["p"], C)
            yield
            for _ in range(n_sq - 1):
                for it in items:
                    both = _packed_dot_split(jnp.concatenate([it["t"], it["p"]], axis=0),
                                             it["p"], C)
                    it["t"] = it["t"] + both[:C]
                    it["p"] = both[C:]
                yield
            for it in items:
                it["t"] = it["t"] + _packed_dot_split(it["t"], it["p"], C)
            yield
        for it in items:
            rhs = jnp.concatenate([_blockdiag_rows(it["vb"].astype(BF16), GDN_DV),
                                   _blockdiag_rows(it["kbe"].astype(BF16), GDN_DK)], axis=1)
            uw = _dot(it["t"].astype(BF16), rhs)
            out.setdefault(it["c"], []).append(
                (uw[:, :W], uw[:, W:], it["qk"], it["qe"], it["k"], it["gcols"]))

    state = [s_ref[h] for h in range(GDN_HEADS)]

    def recurrence(chunks, local):
        for c in chunks:
            rows = slice(c * C, (c + 1) * C)
            groups = local[c]
            r = [[_dot(jnp.concatenate([w[:, hs], qe[:, hs]], axis=0).astype(BF16),
                       state[gi * G + j].astype(BF16))
                  for j, hs in enumerate(head_slices)]
                 for gi, (u, w, qk, qe, k, gcols) in enumerate(groups)]
            yield
            v_new_bf = [[(u[:, hs] - r[gi][j][:C]).astype(BF16)
                         for j, hs in enumerate(head_slices)]
                        for gi, (u, w, qk, qe, k, gcols) in enumerate(groups)]
            o = [jnp.concatenate([r[gi][j][C:] for j in range(G)], axis=1)
                 + _dot(qk.astype(BF16),
                        _blockdiag_rows(jnp.concatenate(v_new_bf[gi], axis=1), GDN_DV))
                 for gi, (u, w, qk, qe, k, gcols) in enumerate(groups)]
            for gi, (u, w, qk, qe, k, gcols) in enumerate(groups):
                for j, hs in enumerate(head_slices):
                    h = gi * G + j
                    g_last = gcols[j][C - 1:C, :]
                    k_dec = (k[:, hs] * jnp.exp(g_last - gcols[j])).astype(BF16)
                    state[h] = state[h] * jnp.exp(g_last) + _tn_dot(k_dec, v_new_bf[gi][j])
            yield
            for gi in range(n_groups):
                for j, hs in enumerate(head_slices):
                    h = gi * G + j
                    oh = o[gi][:, hs]
                    on = oh * lax.rsqrt(jnp.mean(oh * oh, axis=-1, keepdims=True) + RMS_EPS) * nw
                    ocols = slice(h * GDN_DV, (h + 1) * GDN_DV)
                    o_ref[0, rows, ocols] = (on * gate_ref[0, rows, ocols].astype(F32)
                                             ).astype(BF16)
            yield

    def alternate(*gens):
        gens = list(gens)
        done = object()
        while gens:
            for g in list(gens):
                if next(g, done) is done:
                    gens.remove(g)

    sets = [list(range(c0, min(c0 + set_chunks, n_chunks))) for c0 in range(0, n_chunks, set_chunks)]
    local = {}
    alternate(local_part(sets[0], local))
    for prev, cur in zip(sets[:-1], sets[1:]):
        alternate(local_part(cur, local), recurrence(prev, local))
    alternate(recurrence(sets[-1], local))

    for h in range(GDN_HEADS):
        s_ref[h] = state[h]
        sout_ref[0, h] = state[h]


def _gdn(qg, kg, vg, gate, gb, s0, norm_w, chunk, n_chunks):
    B, T, _ = qg.shape
    tb = chunk * n_chunks
    nt = T // tb
    tok_spec = lambda width: pl.BlockSpec((1, tb, width), lambda b, t: (b, t, 0))
    st_spec = pl.BlockSpec((1, GDN_HEADS, GDN_DK, GDN_DV), lambda b, t: (b, 0, 0, 0))
    return pl.pallas_call(
        functools.partial(_gdn_kernel, chunk=chunk, n_chunks=n_chunks,
                          set_chunks=GDN_SET_CHUNKS),
        grid=(B, nt),
        in_specs=[tok_spec(GDN_QK), tok_spec(GDN_QK), tok_spec(GDN_V), tok_spec(GDN_V),
                  tok_spec(LANES), st_spec, pl.BlockSpec((1, GDN_DV), lambda b, t: (0, 0))],
        out_specs=(tok_spec(GDN_V), st_spec),
        out_shape=(jax.ShapeDtypeStruct((B, T, GDN_V), BF16),
                   jax.ShapeDtypeStruct((B, GDN_HEADS, GDN_DK, GDN_DV), F32)),
        scratch_shapes=[pltpu.VMEM((GDN_HEADS, GDN_DK, GDN_DV), F32)],
        compiler_params=pltpu.CompilerParams(
            dimension_semantics=("arbitrary", "arbitrary"),
            vmem_limit_bytes=VMEM_LIMIT_BYTES),
        name="gdn",
    )(qg, kg, vg, gate, gb, s0, norm_w)


def _lane_tile(x, width):
    if width % LANES == 0:
        return jnp.concatenate([x] * (width // LANES), axis=1)
    return x[:, :1]


def _attn_kernel(*refs, bq, bk, bk_wide, cache_bk, unroll, q_pos0, k_pos0, lam_init):
    if cache_bk:
        q_ref, k_ref, v_ref, kc_ref, vc_ref, lam_ref, w_ref, o_ref, m_ref, acc_ref = refs
    else:
        q_ref, k_ref, v_ref, lam_ref, w_ref, o_ref, m_ref, acc_ref = refs
    head = pl.program_id(1)
    qi = pl.program_id(2)
    tk = k_ref.shape[1]

    q = q_ref[0]
    lane = lax.broadcasted_iota(jnp.int32, q.shape, 1)
    zero = jnp.zeros_like(q)
    q_maps = [jnp.where(lane < DIFF_DH, q, zero), jnp.where(lane >= DIFF_DH, q, zero)]
    small = bq < LANES
    parts = [(jnp.concatenate(q_maps, axis=0), (0, 1))] if small else [(q_maps[0], (0,)),
                                                                         (q_maps[1], (1,))]

    q_first = q_pos0 + qi * bq
    q_last = q_first + (bq - 1)
    vis_first = (q_first // CHUNK + 1) * CHUNK - k_pos0
    vis_last = (q_last // CHUNK + 1) * CHUNK - k_pos0
    per_wide = bk_wide // bk
    n_wide = jnp.clip(vis_first // bk_wide, 0, tk // bk_wide)
    n_full = jnp.clip(vis_first // bk, 0, tk // bk)
    n_all = jnp.clip((vis_last + bk - 1) // bk, 0, tk // bk)

    m_ref[...] = jnp.full(m_ref.shape, NEG_INF, F32)
    acc_ref[...] = jnp.zeros(acc_ref.shape, F32)

    def load_new(start, width, ref):
        return ref[0, pl.ds(pl.multiple_of(start, bk), width), :].astype(BF16)

    def load_cache(start, width, ref):
        rows = pl.ds(start * DIFF_HEADS + head, width, stride=DIFF_HEADS)
        return ref[0, rows, :].astype(BF16)

    def steps(i, carry, masked, n, base, width, load, kr, vr):
        starts = [base + (i * n + j) * width for j in range(n)]
        scores = [[_nt_dot(qp, load(st, width, kr)) for qp, _ in parts] for st in starts]
        ones = jnp.ones((width, LANES), BF16)
        m = [jnp.concatenate([m_ref[mp] for mp in maps], axis=0) for _, maps in parts]
        acc = [jnp.concatenate([acc_ref[mp] for mp in maps], axis=0) for _, maps in parts]
        for st, s_parts in zip(starts, scores):
            v_blk = load(st, width, vr)
            if not small:
                v_blk = jnp.concatenate([v_blk, ones], axis=1)
            for pi, s in enumerate(s_parts):
                if masked:
                    kpos = k_pos0 + st + lax.broadcasted_iota(jnp.int32, s.shape, 1)
                    qpos = q_first + lax.broadcasted_iota(jnp.int32, s.shape, 0) % bq
                    s = jnp.where(kpos // CHUNK <= qpos // CHUNK, s, NEG_INF)
                m_new = jnp.maximum(m[pi], jnp.max(s, axis=-1, keepdims=True))
                p = jnp.exp2(s - _lane_tile(m_new, width))
                alpha = jnp.exp2(m[pi] - m_new)
                pv = _dot(p.astype(BF16), v_blk)
                if small:
                    row_sum = jnp.sum(p, axis=-1, keepdims=True)
                    pv = jnp.concatenate([pv, jnp.broadcast_to(row_sum, pv.shape)], axis=1)
                acc[pi] = jnp.concatenate([alpha, alpha], axis=1) * acc[pi] + pv
                m[pi] = m_new
        for pi, (_, maps) in enumerate(parts):
            for r, mp in enumerate(maps):
                m_ref[mp] = m[pi][r * bq:(r + 1) * bq]
                acc_ref[mp] = acc[pi][r * bq:(r + 1) * bq]
        return carry

    def unmasked_run(count, base, width, load, kr, vr, max_unroll):
        done = 0
        u = max_unroll
        while u >= 1:
            trips = (count - done) // u
            lax.fori_loop(0, trips,
                          functools.partial(steps, masked=False, n=u, base=base + done * width,
                                            width=width, load=load, kr=kr, vr=vr), 0)
            done = done + trips * u
            u //= 2

    if cache_bk:
        unmasked_run(kc_ref.shape[1] // (cache_bk * DIFF_HEADS), 0, cache_bk, load_cache, kc_ref,
                     vc_ref, unroll)
    unmasked_run(n_wide, 0, bk_wide, load_new, k_ref, v_ref, unroll)
    unmasked_run(n_full - n_wide * per_wide, n_wide * bk_wide, bk, load_new, k_ref, v_ref, 1)
    lax.fori_loop(0, n_all - n_full,
                  functools.partial(steps, masked=True, n=1, base=n_full * bk, width=bk,
                                    load=load_new, kr=k_ref, vr=v_ref), 0)

    lv = lam_ref[...]
    lam = (jnp.exp(jnp.sum(lv[0:1] * lv[1:2], axis=-1, keepdims=True))
           - jnp.exp(jnp.sum(lv[2:3] * lv[3:4], axis=-1, keepdims=True)) + lam_init)
    o = [acc_ref[mp, :, :DIFF_DV] / acc_ref[mp, :, DIFF_DV:] for mp in range(2)]
    out = o[0] - lam * o[1]
    out = out * lax.rsqrt(jnp.mean(out * out, axis=-1, keepdims=True) + RMS_EPS) * w_ref[...]
    o_ref[0] = (out * (1.0 - lam_init)).astype(BF16)


def _attn(qd, k_new, v_new, k_cache, v_cache, lam_p, subln_w, bq, bk, bk_wide, cache_bk, unroll,
          q_pos0, lam_init):
    B, Tq, _ = qd.shape
    Tk = k_new.shape[1]
    nq = Tq // bq
    kv_spec = pl.BlockSpec((1, Tk, LANES), lambda b, h, i: (b, 0, h))
    in_specs = [pl.BlockSpec((1, bq, LANES), lambda b, h, i: (b, i, h)), kv_spec, kv_spec]
    args = [qd, k_new, v_new]
    k_pos0 = 0
    if k_cache is not None:
        cache_spec = pl.BlockSpec((1,) + k_cache.shape[1:], lambda b, h, i: (b, 0, 0))
        in_specs += [cache_spec, cache_spec]
        args += [k_cache, v_cache]
        k_pos0 = k_cache.shape[1] // DIFF_HEADS
    in_specs += [pl.BlockSpec(lam_p.shape, lambda b, h, i: (0, 0)),
                 pl.BlockSpec(subln_w.shape, lambda b, h, i: (0, 0))]
    args += [lam_p, subln_w]
    return pl.pallas_call(
        functools.partial(_attn_kernel, bq=bq, bk=bk, bk_wide=bk_wide,
                          cache_bk=cache_bk if k_cache is not None else 0, unroll=unroll,
                          q_pos0=q_pos0, k_pos0=k_pos0, lam_init=lam_init),
        grid=(B, DIFF_HEADS, nq),
        in_specs=in_specs,
        out_specs=pl.BlockSpec((1, bq, LANES), lambda b, h, i: (b, i, h)),
        out_shape=jax.ShapeDtypeStruct((B, Tq, DIFF_V), BF16),
        scratch_shapes=[pltpu.VMEM((2, bq, LANES), F32),
                        pltpu.VMEM((2, bq, DIFF_DV + LANES), F32)],
        compiler_params=pltpu.CompilerParams(
            dimension_semantics=("arbitrary", "arbitrary", "arbitrary"),
            vmem_limit_bytes=VMEM_LIMIT_BYTES),
        name="diffattn",
    )(*args)


def _ffn_kernel(x_ref, oa_ref, ob_ref, wo_ref, g1_ref, b1_ref, wup_ref, cw_ref, cb_ref, cbuf_ref,
                wdn_ref, g2_ref, b2_ref, y_ref, cstate_ref, prev_ref, *, alpha, d_ff,
                row_parts):
    t = pl.program_id(1)

    @pl.when(t == 0)
    def _():
        prev_ref[...] = cbuf_ref[0]

    tm = x_ref.shape[1]
    tails = {}

    def conv(up, prev, cols):
        cw = cw_ref[:, cols]
        u = up * cw[FFN_CONV - 1:FFN_CONV] + cb_ref[:, cols]
        for j in range(FFN_CONV - 1):
            u = u + _shifted_rows(prev, up, FFN_CONV - 1 - j) * cw[j:j + 1]
        return u

    def rows_part(idx, r0, n):
        rows = slice(r0, r0 + n)
        halves = (slice(0, d_ff), slice(d_ff, 2 * d_ff))
        mix = _dot(oa_ref[0, rows], wo_ref[:GDN_V, :]) + _dot(ob_ref[0, rows], wo_ref[GDN_V:, :])
        yield
        x1 = _layer_norm(alpha * x_ref[0, rows] + mix, g1_ref[...], b1_ref[...])
        x1b = x1.astype(BF16)
        up = [_dot(x1b, wup_ref[:, cols]) for cols in halves]
        tails[idx] = [u_[n - SUBLANES:] for u_ in up]
        yield
        prev = tails[idx - 1] if idx > 0 else [prev_ref[:, cols] for cols in halves]
        ua, ub = [conv(u_, pv, cols) for u_, pv, cols in zip(up, prev, halves)]
        if r0 + n == tm:
            for tail, cols in zip(tails[idx], halves):
                prev_ref[:, cols] = tail
                cstate_ref[0, :, cols] = tail
        hh = (_silu(ua) * ub).astype(BF16)
        yield
        y = _dot(hh, wdn_ref[...])
        yield
        y_ref[0, rows] = _layer_norm(alpha * x1 + y, g2_ref[...], b2_ref[...])

    n_parts = row_parts if tm % (row_parts * SUBLANES * 2) == 0 else 1
    n = tm // n_parts
    waiting = [rows_part(i, i * n, n) for i in range(n_parts)]
    live = []
    done = object()
    while waiting or live:
        if waiting:
            live.append(waiting.pop(0))
            next(live[-1])
        for g in list(live):
            if next(g, done) is done:
                live.remove(g)


def _ffn(x, oa, ob, w_o, g1, b1, w_up, cw, cb, cbuf8, w_dn, g2, b2, tm, alpha):
    B, T, D = x.shape
    nt = T // tm
    d_ff = w_dn.shape[0]
    tok_spec = lambda width: pl.BlockSpec((1, tm, width), lambda b, t: (b, t, 0))
    full = lambda a: pl.BlockSpec(a.shape, lambda b, t: (0,) * a.ndim)
    once = lambda a: pl.BlockSpec(a.shape, lambda b, t: (0,) * a.ndim,
                                  pipeline_mode=pl.Buffered(1))
    st_spec = pl.BlockSpec((1, SUBLANES, 2 * d_ff), lambda b, t: (b, 0, 0))
    return pl.pallas_call(
        functools.partial(_ffn_kernel, alpha=alpha, d_ff=d_ff, row_parts=FFN_ROW_PARTS),
        grid=(B, nt),
        in_specs=[tok_spec(D), tok_spec(GDN_V), tok_spec(DIFF_V), once(w_o), full(g1), full(b1),
                  once(w_up), full(cw), full(cb), st_spec, once(w_dn), full(g2), full(b2)],
        out_specs=(tok_spec(D), st_spec),
        out_shape=(jax.ShapeDtypeStruct((B, T, D), F32),
                   jax.ShapeDtypeStruct((B, SUBLANES, 2 * d_ff), F32)),
        scratch_shapes=[pltpu.VMEM((SUBLANES, 2 * d_ff), F32)],
        compiler_params=pltpu.CompilerParams(
            dimension_semantics=("arbitrary", "arbitrary"),
            vmem_limit_bytes=VMEM_LIMIT_BYTES),
        name="ffn",
    )(x, oa, ob, w_o, g1, b1, w_up, cw, cb, cbuf8, w_dn, g2, b2)


def _pad_rows_front(buf, rows):
    B, r, C = buf.shape
    return jnp.concatenate([jnp.zeros((B, rows - r, C), buf.dtype), buf], axis=1)


def _lane_row(vec):
    n = vec.shape[0]
    pad = (-n) % LANES
    return jnp.pad(vec.astype(F32), (0, pad)).reshape(1, n + pad)


def _rope_tables(pos0, T):
    half = DIFF_DH // 2
    inv = ROPE_THETA ** (-jnp.arange(half, dtype=F32) * (2.0 / DIFF_DH))
    pos = pos0 + jnp.arange(T, dtype=jnp.int32)
    ang = pos.astype(F32)[:, None] * inv[None, :]
    cos = jnp.cos(ang)
    sin = jnp.sin(ang)
    reps = LANES // DIFF_DH
    cos_t = jnp.tile(jnp.concatenate([cos, cos], axis=-1), (1, reps))
    sin_t = jnp.tile(jnp.concatenate([-sin, sin], axis=-1), (1, reps))
    return cos_t, sin_t


def _largest_divisor(n, cap, mult):
    best = None
    for d in range(mult, min(n, cap) + 1, mult):
        if n % d == 0:
            best = d
    return best if best is not None else n


def _layer(x, pos0, k_past, v_past, s0, conv_qkv_buf, conv_ffn_buf, p, lam_init, alpha):
    B, T, _ = x.shape
    tm = _largest_divisor(T, 1024, SUBLANES)
    cos_t, sin_t = _rope_tables(pos0, T)
    (qg, kg, vg, gate, gb, qd, kd, vd, k_new, v_new, cq8) = _inproj(
        x, p["w_in"], p["gdn_conv_w"], _pad_rows_front(conv_qkv_buf, SUBLANES),
        p["alog"], p["dtb"], cos_t, sin_t, tm)

    chunk = CHUNK if T % CHUNK == 0 else T
    n_chunks = _largest_divisor(T // chunk, GDN_STEP_CHUNKS, 1)
    oa, s_new = _gdn(qg, kg, vg, gate, gb, s0, p["gdn_norm_w"], chunk, n_chunks)

    if k_past is None:
        k_cache = v_cache = None
        cache_bk = 0
    else:
        k_cache = k_past.reshape(B, -1, 2 * DIFF_DH)
        v_cache = v_past.reshape(B, -1, DIFF_DV)
        cache_bk = _largest_divisor(k_past.shape[1], 1024, 16)
    bq = _largest_divisor(T, 512, 16)
    bk = _largest_divisor(T, 512, 16)
    bk_wide = 2 * bk if T % (2 * bk) == 0 else bk
    ob = _attn(qd, kd, vd, k_cache, v_cache, p["diff_lambda"], p["diff_subln_w"], bq, bk,
               bk_wide, cache_bk, 4, pos0, lam_init)

    tm_f = _largest_divisor(T, 512, SUBLANES)
    y, cf8 = _ffn(x, oa, ob, p["w_o"], p["ln1_g"], p["ln1_b"], p["w_up"], p["ffn_conv_w"],
                  p["ffn_conv_b"], _pad_rows_front(conv_ffn_buf, SUBLANES), p["w_down"],
                  p["ln2_g"], p["ln2_b"], tm_f, alpha)

    new_k = k_new.reshape(B, T, DIFF_HEADS, 2 * DIFF_DH)
    new_v = v_new.reshape(B, T, DIFF_HEADS, DIFF_DV)
    return (y, new_k, new_v, s_new, cq8[:, SUBLANES - (GDN_CONV - 1):],
            cf8[:, SUBLANES - (FFN_CONV - 1):])


def _prep_params(l, w_in, gdn_conv_w, gdn_a_log, gdn_dt_bias, gdn_norm_w, diff_lambda,
                 diff_subln_w, w_o, ln1_g, ln1_b, w_up, ffn_conv_w, ffn_conv_b, w_down,
                 ln2_g, ln2_b):
    o1 = GDN_QKV
    o2 = o1 + GDN_V
    o3 = o2 + GDN_HEADS
    o4 = o3 + GDN_HEADS
    w = w_in[l]
    d_model = w.shape[0]
    w_r = jnp.concatenate(
        [w[:, :o2], w[:, o4:], w[:, o2:o4],
         jnp.zeros((d_model, LANES - 2 * GDN_HEADS), w.dtype)], axis=1).astype(BF16)
    row = lambda v: v.astype(F32).reshape(1, -1)
    return {
        "w_in": w_r, "gdn_conv_w": gdn_conv_w[l].astype(F32),
        "alog": _lane_row(gdn_a_log[l]), "dtb": _lane_row(gdn_dt_bias[l]),
        "gdn_norm_w": row(gdn_norm_w[l]), "diff_lambda": diff_lambda[l].astype(F32),
        "diff_subln_w": row(diff_subln_w[l]), "w_o": w_o[l].astype(BF16),
        "ln1_g": row(ln1_g[l]), "ln1_b": row(ln1_b[l]), "w_up": w_up[l].astype(BF16),
        "ffn_conv_w": ffn_conv_w[l].astype(F32), "ffn_conv_b": row(ffn_conv_b[l]),
        "w_down": w_down[l].astype(BF16), "ln2_g": row(ln2_g[l]), "ln2_b": row(ln2_b[l]),
    }


def kernel(x_prompt, x_sample, cache_k, cache_v, state_gdn, state_conv_qkv, state_conv_ffn, w_in, gdn_conv_w, gdn_a_log, gdn_dt_bias, gdn_norm_w, diff_lambda, diff_subln_w, w_o, ln1_g, ln1_b, w_up, ffn_conv_w, ffn_conv_b, w_down, ln2_g, ln2_b):
    depth = w_in.shape[0]
    Bp = x_prompt.shape[0]
    past = cache_k.shape[2]
    alpha = (2 * depth) ** 0.25
    xp, xs = x_prompt, x_sample
    outs_p, outs_s = [], []
    for l in range(depth):
        p = _prep_params(l, w_in, gdn_conv_w, gdn_a_log, gdn_dt_bias, gdn_norm_w, diff_lambda,
                         diff_subln_w, w_o, ln1_g, ln1_b, w_up, ffn_conv_w, ffn_conv_b, w_down,
                         ln2_g, ln2_b)
        lam_init = 0.8 - 0.6 * math.exp(-0.3 * l)
        s0_p = jnp.zeros((Bp, GDN_HEADS, GDN_DK, GDN_DV), F32)
        cq0_p = jnp.zeros((Bp, GDN_CONV - 1, GDN_QKV), xp.dtype)
        cf0_p = jnp.zeros((Bp, FFN_CONV - 1, state_conv_ffn.shape[-1]), xp.dtype)
        xp, kp, vp, sp, cqp, cfp = _layer(xp, 0, None, None, s0_p, cq0_p, cf0_p, p, lam_init, alpha)
        xs, ksn, vsn, ssn, cqs, cfs = _layer(xs, past, cache_k[l], cache_v[l], state_gdn[l],
                                             state_conv_qkv[l], state_conv_ffn[l], p, lam_init,
                                             alpha)
        outs_p.append((kp, vp, sp, cqp, cfp))
        outs_s.append((ksn, vsn, ssn, cqs, cfs))

    def stk(outs, i):
        return jnp.stack([o[i] for o in outs])

    return (xp, xs,
            stk(outs_p, 0), stk(outs_p, 1), stk(outs_p, 2), stk(outs_p, 3), stk(outs_p, 4),
            stk(outs_s, 0), stk(outs_s, 1), stk(outs_s, 2), stk(outs_s, 3), stk(outs_s, 4))
```

```python
import functools
import math

import jax
import jax.numpy as jnp
from jax import lax
from jax.experimental import pallas as pl
from jax.experimental.pallas import tpu as pltpu

F32 = jnp.float32
BF16 = jnp.bfloat16

LANES = 128
SUBLANES = 8
VMEM_LIMIT_BYTES = 56 * 1024 * 1024
FFN_ROW_PARTS = 2
MIN_PART_ROWS = 128
INPROJ_ROW_PARTS = 4
GDN_STEP_CHUNKS = 16
GDN_SET_CHUNKS = 4

CHUNK = 64
GDN_HEADS = 4
GDN_DK = 128
GDN_DV = 128
GDN_CONV = 4
DIFF_HEADS = 4
DIFF_DH = 64
DIFF_DV = 2 * DIFF_DH
ROPE_THETA = 10000.0
FFN_CONV = 3
LN_EPS = 1e-5
RMS_EPS = 1e-6
L2_EPS = 1e-6
NEG_INF = -1e30

GDN_QK = GDN_HEADS * GDN_DK
GDN_V = GDN_HEADS * GDN_DV
GDN_QKV = 2 * GDN_QK + GDN_V
DIFF_QK = DIFF_HEADS * 2 * DIFF_DH
DIFF_V = DIFF_HEADS * DIFF_DV

COL_GATE = GDN_QKV
COL_QB = COL_GATE + GDN_V
COL_KB = COL_QB + DIFF_QK
COL_VB = COL_KB + DIFF_QK
COL_AB = COL_VB + DIFF_V
D_IN_PAD = COL_AB + LANES


def _nt_dot(a, b):
    return lax.dot_general(a, b, (((1,), (1,)), ((), ())), preferred_element_type=F32)


def _tn_dot(a, b):
    return lax.dot_general(a, b, (((0,), (0,)), ((), ())), preferred_element_type=F32)


def _dot(a, b):
    return jnp.dot(a, b, preferred_element_type=F32)


def _sigmoid(x):
    return 1.0 / (1.0 + jnp.exp(-x))


def _silu(x):
    return x * _sigmoid(x)


def _softplus(x):
    return jnp.maximum(x, 0.0) + jnp.log(1.0 + jnp.exp(-jnp.abs(x)))


def _layer_norm(x, g, b):
    mu = jnp.mean(x, axis=-1, keepdims=True)
    xc = x - mu
    var = jnp.mean(xc * xc, axis=-1, keepdims=True)
    return xc * lax.rsqrt(var + LN_EPS) * g + b


def _shifted_rows(prev, cur, back):
    if back == 0:
        return cur
    ext = jnp.concatenate([prev, cur], axis=0)
    n = cur.shape[0]
    return ext[SUBLANES - back:SUBLANES - back + n]


def _inproj_kernel(x_ref, w_ref, convw_ref, cbuf_ref, alog_ref, dtb_ref, cos_ref, sin_ref,
                   qg_ref, kg_ref, vg_ref, gate_ref, gb_ref, qd_ref, kd_ref, vd_ref,
                   kout_ref, vout_ref, cstate_ref, prev_ref, *, row_parts):
    t = pl.program_id(1)

    @pl.when(t == 0)
    def _():
        prev_ref[:SUBLANES, :] = cbuf_ref[0]

    tm = x_ref.shape[1]

    def rows_part(r0, n):
        rows = slice(r0, r0 + n)
        x = x_ref[0, rows].astype(BF16)
        proj = lambda lo, hi: _dot(x, w_ref[:, lo:hi])
        cos = cos_ref[rows]
        sin = sin_ref[rows]

        def rope(z):
            first_half = (lax.broadcasted_iota(jnp.int32, z.shape, 1) % DIFF_DH) < (DIFF_DH // 2)
            swapped = jnp.where(first_half, pltpu.roll(z, LANES - DIFF_DH // 2, 1),
                                pltpu.roll(z, DIFF_DH // 2, 1))
            return z * cos + swapped * sin

        head_rows = lambda hd: pl.ds(hd + r0 * DIFF_HEADS, n, stride=DIFF_HEADS)

        qkv = proj(0, COL_GATE)
        yield
        hm = proj(COL_GATE, COL_KB)
        yield
        cw = convw_ref[...]
        prev_ref[SUBLANES + r0:SUBLANES + r0 + n, :] = qkv
        y = qkv * cw[GDN_CONV - 1:GDN_CONV]
        for j in range(GDN_CONV - 1):
            back = GDN_CONV - 1 - j
            y = y + prev_ref[pl.ds(SUBLANES + r0 - back, n), :] * cw[j:j + 1]
        if r0 + n == tm:
            tail = qkv[n - SUBLANES:]
            prev_ref[:SUBLANES, :] = tail
            cstate_ref[0] = tail
        y = _silu(y)
        for hd in range(GDN_HEADS):
            for base, ref in ((0, qg_ref), (GDN_QK, kg_ref)):
                z = y[:, base + hd * GDN_DK: base + (hd + 1) * GDN_DK]
                z = z * lax.rsqrt(jnp.sum(z * z, axis=-1, keepdims=True) + L2_EPS)
                ref[0, rows, hd * GDN_DK:(hd + 1) * GDN_DK] = z.astype(BF16)
        vg_ref[0, rows] = y[:, 2 * GDN_QK:].astype(BF16)
        yield
        hl = proj(COL_KB, D_IN_PAD)
        yield
        gate_ref[0, rows] = _silu(hm[:, :GDN_V]).astype(BF16)
        scale = DIFF_DH ** -0.5 * math.log2(math.e)
        for hd in range(DIFF_HEADS):
            sl = slice(hd * LANES, (hd + 1) * LANES)
            qd_ref[0, rows, sl] = (rope(hm[:, GDN_V + hd * LANES:GDN_V + (hd + 1) * LANES])
                                   * scale).astype(BF16)
        yield
        for hd in range(DIFF_HEADS):
            sl = slice(hd * LANES, (hd + 1) * LANES)
            r = rope(hl[:, sl])
            kout_ref[0, head_rows(hd), :] = r
            kd_ref[0, rows, sl] = r.astype(BF16)
        vb = hl[:, DIFF_QK:DIFF_QK + DIFF_V]
        for hd in range(DIFF_HEADS):
            vout_ref[0, head_rows(hd), :] = vb[:, hd * DIFF_DV:(hd + 1) * DIFF_DV]
        vd_ref[0, rows] = vb.astype(BF16)
        ab = hl[:, DIFF_QK + DIFF_V:]
        g = -jnp.exp(alog_ref[...]) * _softplus(ab + dtb_ref[...])
        lane = lax.broadcasted_iota(jnp.int32, ab.shape, 1)
        gb_ref[0, rows] = jnp.where(lane < GDN_HEADS, g, _sigmoid(ab))

    n_parts = row_parts if tm % (row_parts * MIN_PART_ROWS) == 0 else 1
    n = tm // n_parts
    waiting = [rows_part(i * n, n) for i in range(n_parts)]
    live = []
    done = object()
    while waiting or live:
        if waiting:
            live.append(waiting.pop(0))
            next(live[-1])
        for g in list(live):
            if next(g, done) is done:
                live.remove(g)


def _inproj(x, w_in_r, conv_w, cbuf8, alog, dtb, cos_t, sin_t, tm):
    B, T, D = x.shape
    nt = T // tm
    tok = lambda width, dt: jax.ShapeDtypeStruct((B, T, width), dt)
    tok_spec = lambda width: pl.BlockSpec((1, tm, width), lambda b, t: (b, t, 0))
    full = lambda a: pl.BlockSpec(a.shape, lambda b, t: (0,) * a.ndim)
    out_shape = (tok(GDN_QK, BF16), tok(GDN_QK, BF16), tok(GDN_V, BF16), tok(GDN_V, BF16),
                 tok(LANES, F32), tok(DIFF_QK, BF16), tok(DIFF_QK, BF16), tok(DIFF_V, BF16),
                 jax.ShapeDtypeStruct((B, T * DIFF_HEADS, 2 * DIFF_DH), F32),
                 jax.ShapeDtypeStruct((B, T * DIFF_HEADS, DIFF_DV), F32),
                 jax.ShapeDtypeStruct((B, SUBLANES, GDN_QKV), F32))
    out_specs = (tok_spec(GDN_QK), tok_spec(GDN_QK), tok_spec(GDN_V), tok_spec(GDN_V),
                 tok_spec(LANES), tok_spec(DIFF_QK), tok_spec(DIFF_QK), tok_spec(DIFF_V),
                 pl.BlockSpec((1, tm * DIFF_HEADS, 2 * DIFF_DH), lambda b, t: (b, t, 0)),
                 pl.BlockSpec((1, tm * DIFF_HEADS, DIFF_DV), lambda b, t: (b, t, 0)),
                 pl.BlockSpec((1, SUBLANES, GDN_QKV), lambda b, t: (b, 0, 0)))
    in_specs = [tok_spec(D), full(w_in_r), full(conv_w),
                pl.BlockSpec((1, SUBLANES, GDN_QKV), lambda b, t: (b, 0, 0)),
                full(alog), full(dtb),
                pl.BlockSpec((tm, LANES), lambda b, t: (t, 0)),
                pl.BlockSpec((tm, LANES), lambda b, t: (t, 0))]
    return pl.pallas_call(
        functools.partial(_inproj_kernel, row_parts=INPROJ_ROW_PARTS),
        grid=(B, nt),
        in_specs=in_specs,
        out_specs=out_specs,
        out_shape=out_shape,
        scratch_shapes=[pltpu.VMEM((SUBLANES + tm, GDN_QKV), F32)],
        compiler_params=pltpu.CompilerParams(
            dimension_semantics=("arbitrary", "arbitrary"),
            vmem_limit_bytes=VMEM_LIMIT_BYTES),
        name="inproj",
    )(x, w_in_r, conv_w, cbuf8, alog, dtb, cos_t, sin_t)


def _split_hi_lo(x):
    hi = x.astype(BF16)
    return hi, (x - hi.astype(F32)).astype(BF16)


def _blockdiag_rows(x, width):
    n = x.shape[1] // width
    blk = lax.broadcasted_iota(jnp.int32, x.shape, 1) // width
    zero = jnp.zeros_like(x)
    return jnp.concatenate([jnp.where(blk == j, x, zero) for j in range(n)], axis=0)


def _packed_dot_split(l2, r2, width):
    l_hi, l_lo = _split_hi_lo(l2)
    bd = _blockdiag_rows(r2.astype(BF16), width)
    return _dot(jnp.concatenate([l_hi, l_lo], axis=1), jnp.concatenate([bd, bd], axis=0))


def _per_head(cols, lane_head):
    out = cols[0]
    for j in range(1, len(cols)):
        out = jnp.where(lane_head >= j, cols[j], out)
    return out


def _gdn_kernel(q_ref, k_ref, v_ref, gate_ref, gb_ref, s0_ref, nw_ref,
                o_ref, sout_ref, s_ref, *, chunk, n_chunks, set_chunks):
    t = pl.program_id(1)

    @pl.when(t == 0)
    def _():
        s_ref[...] = s0_ref[0]

    C = chunk
    G = LANES // C
    W = G * GDN_DK
    n_groups = GDN_HEADS // G
    row = lax.broadcasted_iota(jnp.int32, (C, LANES), 0)
    lane = lax.broadcasted_iota(jnp.int32, (C, LANES), 1)
    lane_in = lane % C
    lane_head_p = lane // C
    lane_head_w = lax.broadcasted_iota(jnp.int32, (C, W), 1) // GDN_DK
    incl = row >= lane_in
    strict = row > lane_in
    eye_b = row == lane_in
    eye = eye_b.astype(F32)
    r1 = lax.broadcasted_iota(jnp.int32, (C, C), 0)
    c1 = lax.broadcasted_iota(jnp.int32, (C, C), 1)
    tri = (r1 >= c1).astype(BF16)
    tri3 = jnp.concatenate([tri, tri, tri], axis=1)
    scale = GDN_DK ** -0.5
    nw = nw_ref[...]
    n_sq = max(int(math.ceil(math.log2(C))) - 1, 0)

    head_slices = [slice(j * GDN_DK, (j + 1) * GDN_DK) for j in range(G)]

    def local_part(chunks, out):
        items = []
        for c in chunks:
            rows = slice(c * C, (c + 1) * C)
            gbc = gb_ref[0, rows, :]
            g_hi = gbc.astype(BF16)
            g_rest = gbc - g_hi.astype(F32)
            g_mid = g_rest.astype(BF16)
            g_lo = (g_rest - g_mid.astype(F32)).astype(BF16)
            gc_all = _dot(tri3, jnp.concatenate([g_hi, g_mid, g_lo], axis=0))
            for gi in range(n_groups):
                heads = [gi * G + j for j in range(G)]
                cols = slice(gi * W, (gi + 1) * W)
                gcols = [gc_all[:, h:h + 1] for h in heads]
                bcols = [gbc[:, GDN_HEADS + h:GDN_HEADS + h + 1] for h in heads]
                gcol_p = _per_head(gcols, lane_head_p)
                grow_p = jnp.sum(jnp.where(eye_b, gcol_p, 0.0), axis=0, keepdims=True)
                decay = jnp.exp(jnp.where(incl, gcol_p - grow_p, -jnp.inf))
                b_w = _per_head(bcols, lane_head_w)
                eg_w = _per_head([jnp.exp(g) for g in gcols], lane_head_w)
                q = q_ref[0, rows, cols].astype(F32) * scale
                k = k_ref[0, rows, cols].astype(F32)
                v = v_ref[0, rows, cols].astype(F32)
                kb = k * b_w
                items.append(dict(c=c, q=q, k=k, kb=kb, vb=v * b_w, kbe=kb * eg_w, qe=q * eg_w,
                                  decay=decay, gcols=gcols))
        yield
        for it in items:
            k_bd = _blockdiag_rows(it["k"].astype(BF16), GDN_DK)
            aq = _nt_dot(jnp.concatenate([it["kb"].astype(BF16), it["q"].astype(BF16)], axis=0),
                         k_bd)
            it["qk"] = aq[C:] * it["decay"]
            it["p"] = -jnp.where(strict, aq[:C] * it["decay"], 0.0)
            it["t"] = eye + it["p"]
        yield
        if n_sq > 0:
            for it in items:
                it["p"] = _packed_dot_split(it["p"], it["p"], C)
            yield
            for _ in range(n_sq - 1):
                for it in items:
                    both = _packed_dot_split(jnp.concatenate([it["t"], it["p"]], axis=0),
                                             it["p"], C)
                    it["t"] = it["t"] + both[:C]
                    it["p"] = both[C:]
                yield
            for it in items:
                it["t"] = it["t"] + _packed_dot_split(it["t"], it["p"], C)
            yield
        for it in items:
            rhs = jnp.concatenate([_blockdiag_rows(it["vb"].astype(BF16), GDN_DV),
                                   _blockdiag_rows(it["kbe"].astype(BF16), GDN_DK)], axis=1)
            uw = _dot(it["t"].astype(BF16), rhs)
            out.setdefault(it["c"], []).append(
                (uw[:, :W], uw[:, W:], it["qk"], it["qe"], it["k"], it["gcols"]))

    state = [s_ref[h] for h in range(GDN_HEADS)]

    def recurrence(chunks, local):
        for c in chunks:
            rows = slice(c * C, (c + 1) * C)
            groups = local[c]
            r = [[_dot(jnp.concatenate([w[:, hs], qe[:, hs]], axis=0).astype(BF16),
                       state[gi * G + j].astype(BF16))
                  for j, hs in enumerate(head_slices)]
                 for gi, (u, w, qk, qe, k, gcols) in enumerate(groups)]
            yield
            v_new_bf = [[(u[:, hs] - r[gi][j][:C]).astype(BF16)
                         for j, hs in enumerate(head_slices)]
                        for gi, (u, w, qk, qe, k, gcols) in enumerate(groups)]
            o = [jnp.concatenate([r[gi][j][C:] for j in range(G)], axis=1)
                 + _dot(qk.astype(BF16),
                        _blockdiag_rows(jnp.concatenate(v_new_bf[gi], axis=1), GDN_DV))
                 for gi, (u, w, qk, qe, k, gcols) in enumerate(groups)]
            for gi, (u, w, qk, qe, k, gcols) in enumerate(groups):
                for j, hs in enumerate(head_slices):
                    h = gi * G + j
                    g_last = gcols[j][C - 1:C, :]
                    k_dec = (k[:, hs] * jnp.exp(g_last - gcols[j])).astype(BF16)
                    state[h] = state[h] * jnp.exp(g_last) + _tn_dot(k_dec, v_new_bf[gi][j])
            yield
            for gi in range(n_groups):
                for j, hs in enumerate(head_slices):
                    h = gi * G + j
                    oh = o[gi][:, hs]
                    on = oh * lax.rsqrt(jnp.mean(oh * oh, axis=-1, keepdims=True) + RMS_EPS) * nw
                    ocols = slice(h * GDN_DV, (h + 1) * GDN_DV)
                    o_ref[0, rows, ocols] = (on * gate_ref[0, rows, ocols].astype(F32)
                                             ).astype(BF16)
            yield

    def alternate(*gens):
        gens = list(gens)
        done = object()
        while gens:
            for g in list(gens):
                if next(g, done) is done:
                    gens.remove(g)

    sets = [list(range(c0, min(c0 + set_chunks, n_chunks))) for c0 in range(0, n_chunks, set_chunks)]
    local = {}
    alternate(local_part(sets[0], local))
    for prev, cur in zip(sets[:-1], sets[1:]):
        alternate(local_part(cur, local), recurrence(prev, local))
    alternate(recurrence(sets[-1], local))

    for h in range(GDN_HEADS):
        s_ref[h] = state[h]
        sout_ref[0, h] = state[h]


def _gdn(qg, kg, vg, gate, gb, s0, norm_w, chunk, n_chunks):
    B, T, _ = qg.shape
    tb = chunk * n_chunks
    nt = T // tb
    tok_spec = lambda width: pl.BlockSpec((1, tb, width), lambda b, t: (b, t, 0))
    st_spec = pl.BlockSpec((1, GDN_HEADS, GDN_DK, GDN_DV), lambda b, t: (b, 0, 0, 0))
    return pl.pallas_call(
        functools.partial(_gdn_kernel, chunk=chunk, n_chunks=n_chunks,
                          set_chunks=GDN_SET_CHUNKS),
        grid=(B, nt),
        in_specs=[tok_spec(GDN_QK), tok_spec(GDN_QK), tok_spec(GDN_V), tok_spec(GDN_V),
                  tok_spec(LANES), st_spec, pl.BlockSpec((1, GDN_DV), lambda b, t: (0, 0))],
        out_specs=(tok_spec(GDN_V), st_spec),
        out_shape=(jax.ShapeDtypeStruct((B, T, GDN_V), BF16),
                   jax.ShapeDtypeStruct((B, GDN_HEADS, GDN_DK, GDN_DV), F32)),
        scratch_shapes=[pltpu.VMEM((GDN_HEADS, GDN_DK, GDN_DV), F32)],
        compiler_params=pltpu.CompilerParams(
            dimension_semantics=("arbitrary", "arbitrary"),
            vmem_limit_bytes=VMEM_LIMIT_BYTES),
        name="gdn",
    )(qg, kg, vg, gate, gb, s0, norm_w)


def _lane_tile(x, width):
    if width % LANES == 0:
        return jnp.concatenate([x] * (width // LANES), axis=1)
    return x[:, :1]


def _attn_kernel(*refs, bq, bk, bk_wide, cache_bk, unroll, q_pos0, k_pos0, lam_init):
    if cache_bk:
        q_ref, k_ref, v_ref, kc_ref, vc_ref, lam_ref, w_ref, o_ref, m_ref, acc_ref = refs
    else:
        q_ref, k_ref, v_ref, lam_ref, w_ref, o_ref, m_ref, acc_ref = refs
    head = pl.program_id(1)
    qi = pl.program_id(2)
    tk = k_ref.shape[1]

    q = q_ref[0]
    lane = lax.broadcasted_iota(jnp.int32, q.shape, 1)
    zero = jnp.zeros_like(q)
    q_maps = [jnp.where(lane < DIFF_DH, q, zero), jnp.where(lane >= DIFF_DH, q, zero)]
    small = bq < LANES
    parts = [(jnp.concatenate(q_maps, axis=0), (0, 1))] if small else [(q_maps[0], (0,)),
                                                                         (q_maps[1], (1,))]

    q_first = q_pos0 + qi * bq
    q_last = q_first + (bq - 1)
    vis_first = (q_first // CHUNK + 1) * CHUNK - k_pos0
    vis_last = (q_last // CHUNK + 1) * CHUNK - k_pos0
    per_wide = bk_wide // bk
    n_wide = jnp.clip(vis_first // bk_wide, 0, tk // bk_wide)
    n_full = jnp.clip(vis_first // bk, 0, tk // bk)
    n_all = jnp.clip((vis_last + bk - 1) // bk, 0, tk // bk)

    m_ref[...] = jnp.full(m_ref.shape, NEG_INF, F32)
    acc_ref[...] = jnp.zeros(acc_ref.shape, F32)

    def load_new(start, width, ref):
        return ref[0, pl.ds(pl.multiple_of(start, bk), width), :].astype(BF16)

    def load_cache(start, width, ref):
        rows = pl.ds(start * DIFF_HEADS + head, width, stride=DIFF_HEADS)
        return ref[0, rows, :].astype(BF16)

    def process(blocks, load, kr, vr):
        scores = [[_nt_dot(qp, load(st, width, kr)) for qp, _ in parts] for st, width, _ in blocks]
        m = [jnp.concatenate([m_ref[mp] for mp in maps], axis=0) for _, maps in parts]
        acc = [jnp.concatenate([acc_ref[mp] for mp in maps], axis=0) for _, maps in parts]
        for (st, width, masked), s_parts in zip(blocks, scores):
            v_blk = load(st, width, vr)
            if not small:
                v_blk = jnp.concatenate([v_blk, jnp.ones((width, LANES), BF16)], axis=1)
            for pi, s in enumerate(s_parts):
                if masked:
                    kpos = k_pos0 + st + lax.broadcasted_iota(jnp.int32, s.shape, 1)
                    qpos = q_first + lax.broadcasted_iota(jnp.int32, s.shape, 0) % bq
                    s = jnp.where(kpos // CHUNK <= qpos // CHUNK, s, NEG_INF)
                m_new = jnp.maximum(m[pi], jnp.max(s, axis=-1, keepdims=True))
                p = jnp.exp2(s - _lane_tile(m_new, width))
                alpha = jnp.exp2(m[pi] - m_new)
                pv = _dot(p.astype(BF16), v_blk)
                if small:
                    row_sum = jnp.sum(p, axis=-1, keepdims=True)
                    pv = jnp.concatenate([pv, jnp.broadcast_to(row_sum, pv.shape)], axis=1)
                acc[pi] = jnp.concatenate([alpha, alpha], axis=1) * acc[pi] + pv
                m[pi] = m_new
        for pi, (_, maps) in enumerate(parts):
            for r, mp in enumerate(maps):
                m_ref[mp] = m[pi][r * bq:(r + 1) * bq]
                acc_ref[mp] = acc[pi][r * bq:(r + 1) * bq]

    def steps(i, carry, masked, n, base, width, load, kr, vr):
        process([(base + (i * n + j) * width, width, masked) for j in range(n)], load, kr, vr)
        return carry

    def unmasked_run(count, base, width, load, kr, vr, max_unroll, min_unroll=1):
        done = 0
        u = max_unroll
        while u >= min_unroll:
            trips = (count - done) // u
            lax.fori_loop(0, trips,
                          functools.partial(steps, masked=False, n=u, base=base + done * width,
                                            width=width, load=load, kr=kr, vr=vr), 0)
            done = done + trips * u
            u //= 2
        return done

    if cache_bk:
        unmasked_run(kc_ref.shape[1] // (cache_bk * DIFF_HEADS), 0, cache_bk, load_cache, kc_ref,
                     vc_ref, unroll)
    wide_done = unmasked_run(n_wide, 0, bk_wide, load_new, k_ref, v_ref, unroll,
                             min(2, unroll))
    rem_wide = n_wide - wide_done
    rem_narrow = n_full - n_wide * per_wide
    n_masked = n_all - n_full
    wide_at = wide_done * bk_wide
    narrow_at = n_wide * bk_wide
    masked_at = n_full * bk
    new_kv = dict(load=load_new, kr=k_ref, vr=v_ref)
    tails = [((1, 1), [(wide_at, bk_wide, False), (narrow_at, bk, False), (masked_at, bk, True)]),
             ((1, 0), [(wide_at, bk_wide, False), (masked_at, bk, True)]),
             ((0, 1), [(narrow_at, bk, False), (masked_at, bk, True)])]
    fused = (n_masked == 1) & (rem_wide + rem_narrow >= 1) & (rem_wide <= 1) & (rem_narrow <= 1)
    for (rw, rn), blocks in tails:
        @pl.when(fused & (rem_wide == rw) & (rem_narrow == rn))
        def _(blocks=blocks):
            process(blocks, **new_kv)
    single = lambda cnt: jnp.where(fused, 0, cnt)
    lax.fori_loop(0, single(rem_wide),
                  functools.partial(steps, masked=False, n=1, base=wide_at, width=bk_wide,
                                    **new_kv), 0)
    lax.fori_loop(0, single(rem_narrow),
                  functools.partial(steps, masked=False, n=1, base=narrow_at, width=bk, **new_kv),
                  0)
    lax.fori_loop(0, single(n_masked),
                  functools.partial(steps, masked=True, n=1, base=masked_at, width=bk, **new_kv), 0)

    lv = lam_ref[...]
    lam = (jnp.exp(jnp.sum(lv[0:1] * lv[1:2], axis=-1, keepdims=True))
           - jnp.exp(jnp.sum(lv[2:3] * lv[3:4], axis=-1, keepdims=True)) + lam_init)
    o = [acc_ref[mp, :, :DIFF_DV] / acc_ref[mp, :, DIFF_DV:] for mp in range(2)]
    out = o[0] - lam * o[1]
    out = out * lax.rsqrt(jnp.mean(out * out, axis=-1, keepdims=True) + RMS_EPS) * w_ref[...]
    o_ref[0] = (out * (1.0 - lam_init)).astype(BF16)


def _attn(qd, k_new, v_new, k_cache, v_cache, lam_p, subln_w, bq, bk, bk_wide, cache_bk, unroll,
          q_pos0, lam_init):
    B, Tq, _ = qd.shape
    Tk = k_new.shape[1]
    nq = Tq // bq
    kv_spec = pl.BlockSpec((1, Tk, LANES), lambda b, h, i: (b, 0, h))
    in_specs = [pl.BlockSpec((1, bq, LANES), lambda b, h, i: (b, i, h)), kv_spec, kv_spec]
    args = [qd, k_new, v_new]
    k_pos0 = 0
    if k_cache is not None:
        cache_spec = pl.BlockSpec((1,) + k_cache.shape[1:], lambda b, h, i: (b, 0, 0))
        in_specs += [cache_spec, cache_spec]
        args += [k_cache, v_cache]
        k_pos0 = k_cache.shape[1] // DIFF_HEADS
    in_specs += [pl.BlockSpec(lam_p.shape, lambda b, h, i: (0, 0)),
                 pl.BlockSpec(subln_w.shape, lambda b, h, i: (0, 0))]
    args += [lam_p, subln_w]
    return pl.pallas_call(
        functools.partial(_attn_kernel, bq=bq, bk=bk, bk_wide=bk_wide,
                          cache_bk=cache_bk if k_cache is not None else 0, unroll=unroll,
                          q_pos0=q_pos0, k_pos0=k_pos0, lam_init=lam_init),
        grid=(B, DIFF_HEADS, nq),
        in_specs=in_specs,
        out_specs=pl.BlockSpec((1, bq, LANES), lambda b, h, i: (b, i, h)),
        out_shape=jax.ShapeDtypeStruct((B, Tq, DIFF_V), BF16),
        scratch_shapes=[pltpu.VMEM((2, bq, LANES), F32),
                        pltpu.VMEM((2, bq, DIFF_DV + LANES), F32)],
        compiler_params=pltpu.CompilerParams(
            dimension_semantics=("arbitrary", "arbitrary", "arbitrary"),
            vmem_limit_bytes=VMEM_LIMIT_BYTES),
        name="diffattn",
    )(*args)


def _ffn_kernel(x_ref, oa_ref, ob_ref, wo_ref, g1_ref, b1_ref, wup_ref, cw_ref, cb_ref, cbuf_ref,
                wdn_ref, g2_ref, b2_ref, y_ref, cstate_ref, prev_ref, *, alpha, d_ff,
                row_parts):
    t = pl.program_id(1)

    @pl.when(t == 0)
    def _():
        prev_ref[...] = cbuf_ref[0]

    tm = x_ref.shape[1]
    tails = {}

    def conv(up, prev, cols):
        cw = cw_ref[:, cols]
        u = up * cw[FFN_CONV - 1:FFN_CONV] + cb_ref[:, cols]
        for j in range(FFN_CONV - 1):
            u = u + _shifted_rows(prev, up, FFN_CONV - 1 - j) * cw[j:j + 1]
        return u

    def rows_part(idx, r0, n):
        rows = slice(r0, r0 + n)
        halves = (slice(0, d_ff), slice(d_ff, 2 * d_ff))
        mix = _dot(oa_ref[0, rows], wo_ref[:GDN_V, :]) + _dot(ob_ref[0, rows], wo_ref[GDN_V:, :])
        yield
        x1 = _layer_norm(alpha * x_ref[0, rows] + mix, g1_ref[...], b1_ref[...])
        x1b = x1.astype(BF16)
        up = [_dot(x1b, wup_ref[:, cols]) for cols in halves]
        tails[idx] = [u_[n - SUBLANES:] for u_ in up]
        yield
        prev = tails[idx - 1] if idx > 0 else [prev_ref[:, cols] for cols in halves]
        ua, ub = [conv(u_, pv, cols) for u_, pv, cols in zip(up, prev, halves)]
        if r0 + n == tm:
            for tail, cols in zip(tails[idx], halves):
                prev_ref[:, cols] = tail
                cstate_ref[0, :, cols] = tail
        hh = (_silu(ua) * ub).astype(BF16)
        yield
        y = _dot(hh, wdn_ref[...])
        yield
        y_ref[0, rows] = _layer_norm(alpha * x1 + y, g2_ref[...], b2_ref[...])

    n_parts = row_parts if tm % (row_parts * MIN_PART_ROWS) == 0 else 1
    n = tm // n_parts
    waiting = [rows_part(i, i * n, n) for i in range(n_parts)]
    live = []
    done = object()
    while waiting or live:
        if waiting:
            live.append(waiting.pop(0))
            next(live[-1])
        for g in list(live):
            if next(g, done) is done:
                live.remove(g)


def _ffn(x, oa, ob, w_o, g1, b1, w_up, cw, cb, cbuf8, w_dn, g2, b2, tm, alpha):
    B, T, D = x.shape
    nt = T // tm
    d_ff = w_dn.shape[0]
    tok_spec = lambda width: pl.BlockSpec((1, tm, width), lambda b, t: (b, t, 0))
    full = lambda a: pl.BlockSpec(a.shape, lambda b, t: (0,) * a.ndim)
    once = lambda a: pl.BlockSpec(a.shape, lambda b, t: (0,) * a.ndim,
                                  pipeline_mode=pl.Buffered(1))
    st_spec = pl.BlockSpec((1, SUBLANES, 2 * d_ff), lambda b, t: (b, 0, 0))
    return pl.pallas_call(
        functools.partial(_ffn_kernel, alpha=alpha, d_ff=d_ff, row_parts=FFN_ROW_PARTS),
        grid=(B, nt),
        in_specs=[tok_spec(D), tok_spec(GDN_V), tok_spec(DIFF_V), once(w_o), full(g1), full(b1),
                  once(w_up), full(cw), full(cb), st_spec, once(w_dn), full(g2), full(b2)],
        out_specs=(tok_spec(D), st_spec),
        out_shape=(jax.ShapeDtypeStruct((B, T, D), F32),
                   jax.ShapeDtypeStruct((B, SUBLANES, 2 * d_ff), F32)),
        scratch_shapes=[pltpu.VMEM((SUBLANES, 2 * d_ff), F32)],
        compiler_params=pltpu.CompilerParams(
            dimension_semantics=("arbitrary", "arbitrary"),
            vmem_limit_bytes=VMEM_LIMIT_BYTES),
        name="ffn",
    )(x, oa, ob, w_o, g1, b1, w_up, cw, cb, cbuf8, w_dn, g2, b2)


def _pad_rows_front(buf, rows):
    B, r, C = buf.shape
    return jnp.concatenate([jnp.zeros((B, rows - r, C), buf.dtype), buf], axis=1)


def _lane_row(vec):
    n = vec.shape[0]
    pad = (-n) % LANES
    return jnp.pad(vec.astype(F32), (0, pad)).reshape(1, n + pad)


def _rope_tables(pos0, T):
    half = DIFF_DH // 2
    inv = ROPE_THETA ** (-jnp.arange(half, dtype=F32) * (2.0 / DIFF_DH))
    pos = pos0 + jnp.arange(T, dtype=jnp.int32)
    ang = pos.astype(F32)[:, None] * inv[None, :]
    cos = jnp.cos(ang)
    sin = jnp.sin(ang)
    reps = LANES // DIFF_DH
    cos_t = jnp.tile(jnp.concatenate([cos, cos], axis=-1), (1, reps))
    sin_t = jnp.tile(jnp.concatenate([-sin, sin], axis=-1), (1, reps))
    return cos_t, sin_t


def _largest_divisor(n, cap, mult):
    best = None
    for d in range(mult, min(n, cap) + 1, mult):
        if n % d == 0:
            best = d
    return best if best is not None else n


def _layer(x, pos0, k_past, v_past, s0, conv_qkv_buf, conv_ffn_buf, p, lam_init, alpha):
    B, T, _ = x.shape
    tm = _largest_divisor(T, 1024, SUBLANES)
    cos_t, sin_t = _rope_tables(pos0, T)
    (qg, kg, vg, gate, gb, qd, kd, vd, k_new, v_new, cq8) = _inproj(
        x, p["w_in"], p["gdn_conv_w"], _pad_rows_front(conv_qkv_buf, SUBLANES),
        p["alog"], p["dtb"], cos_t, sin_t, tm)

    chunk = CHUNK if T % CHUNK == 0 else T
    n_chunks = _largest_divisor(T // chunk, GDN_STEP_CHUNKS, 1)
    oa, s_new = _gdn(qg, kg, vg, gate, gb, s0, p["gdn_norm_w"], chunk, n_chunks)

    if k_past is None:
        k_cache = v_cache = None
        cache_bk = 0
    else:
        k_cache = k_past.reshape(B, -1, 2 * DIFF_DH)
        v_cache = v_past.reshape(B, -1, DIFF_DV)
        cache_bk = _largest_divisor(k_past.shape[1], 1024, 16)
    bq = _largest_divisor(T, 512, 16)
    bk = _largest_divisor(T, 512, 16)
    bk_wide = 2 * bk if T % (2 * bk) == 0 else bk
    ob = _attn(qd, kd, vd, k_cache, v_cache, p["diff_lambda"], p["diff_subln_w"], bq, bk,
               bk_wide, cache_bk, 4, pos0, lam_init)

    tm_f = _largest_divisor(T, 512, SUBLANES)
    y, cf8 = _ffn(x, oa, ob, p["w_o"], p["ln1_g"], p["ln1_b"], p["w_up"], p["ffn_conv_w"],
                  p["ffn_conv_b"], _pad_rows_front(conv_ffn_buf, SUBLANES), p["w_down"],
                  p["ln2_g"], p["ln2_b"], tm_f, alpha)

    new_k = k_new.reshape(B, T, DIFF_HEADS, 2 * DIFF_DH)
    new_v = v_new.reshape(B, T, DIFF_HEADS, DIFF_DV)
    return (y, new_k, new_v, s_new, cq8[:, SUBLANES - (GDN_CONV - 1):],
            cf8[:, SUBLANES - (FFN_CONV - 1):])


def _prep_params(l, w_in, gdn_conv_w, gdn_a_log, gdn_dt_bias, gdn_norm_w, diff_lambda,
                 diff_subln_w, w_o, ln1_g, ln1_b, w_up, ffn_conv_w, ffn_conv_b, w_down,
                 ln2_g, ln2_b):
    o1 = GDN_QKV
    o2 = o1 + GDN_V
    o3 = o2 + GDN_HEADS
    o4 = o3 + GDN_HEADS
    w = w_in[l]
    d_model = w.shape[0]
    w_r = jnp.concatenate(
        [w[:, :o2], w[:, o4:], w[:, o2:o4],
         jnp.zeros((d_model, LANES - 2 * GDN_HEADS), w.dtype)], axis=1).astype(BF16)
    row = lambda v: v.astype(F32).reshape(1, -1)
    return {
        "w_in": w_r, "gdn_conv_w": gdn_conv_w[l].astype(F32),
        "alog": _lane_row(gdn_a_log[l]), "dtb": _lane_row(gdn_dt_bias[l]),
        "gdn_norm_w": row(gdn_norm_w[l]), "diff_lambda": diff_lambda[l].astype(F32),
        "diff_subln_w": row(diff_subln_w[l]), "w_o": w_o[l].astype(BF16),
        "ln1_g": row(ln1_g[l]), "ln1_b": row(ln1_b[l]), "w_up": w_up[l].astype(BF16),
        "ffn_conv_w": ffn_conv_w[l].astype(F32), "ffn_conv_b": row(ffn_conv_b[l]),
        "w_down": w_down[l].astype(BF16), "ln2_g": row(ln2_g[l]), "ln2_b": row(ln2_b[l]),
    }


def kernel(x_prompt, x_sample, cache_k, cache_v, state_gdn, state_conv_qkv, state_conv_ffn, w_in, gdn_conv_w, gdn_a_log, gdn_dt_bias, gdn_norm_w, diff_lambda, diff_subln_w, w_o, ln1_g, ln1_b, w_up, ffn_conv_w, ffn_conv_b, w_down, ln2_g, ln2_b):
    depth = w_in.shape[0]
    Bp = x_prompt.shape[0]
    past = cache_k.shape[2]
    alpha = (2 * depth) ** 0.25
    xp, xs = x_prompt, x_sample
    outs_p, outs_s = [], []
    for l in range(depth):
        p = _prep_params(l, w_in, gdn_conv_w, gdn_a_log, gdn_dt_bias, gdn_norm_w, diff_lambda,
                         diff_subln_w, w_o, ln1_g, ln1_b, w_up, ffn_conv_w, ffn_conv_b, w_down,
                         ln2_g, ln2_b)
        lam_init = 0.8 - 0.6 * math.exp(-0.3 * l)
        s0_p = jnp.zeros((Bp, GDN_HEADS, GDN_DK, GDN_DV), F32)
        cq0_p = jnp.zeros((Bp, GDN_CONV - 1, GDN_QKV), xp.dtype)
        cf0_p = jnp.zeros((Bp, FFN_CONV - 1, state_conv_ffn.shape[-1]), xp.dtype)
        xp, kp, vp, sp, cqp, cfp = _layer(xp, 0, None, None, s0_p, cq0_p, cf0_p, p, lam_init, alpha)
        xs, ksn, vsn, ssn, cqs, cfs = _layer(xs, past, cache_k[l], cache_v[l], state_gdn[l],
                                             state_conv_qkv[l], state_conv_ffn[l], p, lam_init,
                                             alpha)
        outs_p.append((kp, vp, sp, cqp, cfp))
        outs_s.append((ksn, vsn, ssn, cqs, cfs))

    def stk(outs, i):
        return jnp.stack([o[i] for o in outs])

    return (xp, xs,
            stk(outs_p, 0), stk(outs_p, 1), stk(outs_p, 2), stk(outs_p, 3), stk(outs_p, 4),
            stk(outs_s, 0), stk(outs_s, 1), stk(outs_s, 2), stk(outs_s, 3), stk(outs_s, 4))
```

```python
import functools
import math

import jax
import jax.numpy as jnp
from jax import lax
from jax.experimental import pallas as pl
from jax.experimental.pallas import tpu as pltpu

F32 = jnp.float32
BF16 = jnp.bfloat16

LANES = 128
SUBLANES = 8
VMEM_LIMIT_BYTES = 56 * 1024 * 1024
FFN_ROW_PARTS = 2
MIN_PART_ROWS = 128
INPROJ_ROW_PARTS = 4
GDN_STEP_CHUNKS = 32
GDN_SET_CHUNKS = 4

CHUNK = 64
GDN_HEADS = 4
GDN_DK = 128
GDN_DV = 128
GDN_CONV = 4
DIFF_HEADS = 4
DIFF_DH = 64
DIFF_DV = 2 * DIFF_DH
ROPE_THETA = 10000.0
FFN_CONV = 3
LN_EPS = 1e-5
RMS_EPS = 1e-6
L2_EPS = 1e-6
NEG_INF = -1e30

GDN_QK = GDN_HEADS * GDN_DK
GDN_V = GDN_HEADS * GDN_DV
GDN_QKV = 2 * GDN_QK + GDN_V
DIFF_QK = DIFF_HEADS * 2 * DIFF_DH
DIFF_V = DIFF_HEADS * DIFF_DV

COL_GATE = GDN_QKV
COL_QB = COL_GATE + GDN_V
COL_KB = COL_QB + DIFF_QK
COL_VB = COL_KB + DIFF_QK
COL_AB = COL_VB + DIFF_V
D_IN_PAD = COL_AB + LANES


def _nt_dot(a, b):
    return lax.dot_general(a, b, (((1,), (1,)), ((), ())), preferred_element_type=F32)


def _tn_dot(a, b):
    return lax.dot_general(a, b, (((0,), (0,)), ((), ())), preferred_element_type=F32)


def _dot(a, b):
    return jnp.dot(a, b, preferred_element_type=F32)


def _sigmoid(x):
    return 1.0 / (1.0 + jnp.exp(-x))


def _silu(x):
    return x * _sigmoid(x)


def _softplus(x):
    return jnp.maximum(x, 0.0) + jnp.log(1.0 + jnp.exp(-jnp.abs(x)))


def _layer_norm(x, g, b):
    mu = jnp.mean(x, axis=-1, keepdims=True)
    xc = x - mu
    var = jnp.mean(xc * xc, axis=-1, keepdims=True)
    return xc * lax.rsqrt(var + LN_EPS) * g + b


def _shifted_rows(prev, cur, back):
    if back == 0:
        return cur
    ext = jnp.concatenate([prev, cur], axis=0)
    n = cur.shape[0]
    return ext[SUBLANES - back:SUBLANES - back + n]


def _inproj_kernel(x_ref, w_ref, convw_ref, cbuf_ref, alog_ref, dtb_ref, cos_ref, sin_ref,
                   qg_ref, kg_ref, vg_ref, gate_ref, gb_ref, qd_ref, kd_ref, vd_ref,
                   kout_ref, vout_ref, cstate_ref, prev_ref, *, row_parts):
    t = pl.program_id(1)

    @pl.when(t == 0)
    def _():
        prev_ref[:SUBLANES, :] = cbuf_ref[0]

    tm = x_ref.shape[1]

    def rows_part(r0, n):
        rows = slice(r0, r0 + n)
        x = x_ref[0, rows].astype(BF16)
        proj = lambda lo, hi: _dot(x, w_ref[:, lo:hi])
        cos = cos_ref[rows]
        sin = sin_ref[rows]

        def rope(z):
            first_half = (lax.broadcasted_iota(jnp.int32, z.shape, 1) % DIFF_DH) < (DIFF_DH // 2)
            swapped = jnp.where(first_half, pltpu.roll(z, LANES - DIFF_DH // 2, 1),
                                pltpu.roll(z, DIFF_DH // 2, 1))
            return z * cos + swapped * sin

        head_rows = lambda hd: pl.ds(hd + r0 * DIFF_HEADS, n, stride=DIFF_HEADS)

        qkv = proj(0, COL_GATE)
        yield
        hm = proj(COL_GATE, COL_KB)
        yield
        cw = convw_ref[...]
        prev_ref[SUBLANES + r0:SUBLANES + r0 + n, :] = qkv
        y = qkv * cw[GDN_CONV - 1:GDN_CONV]
        for j in range(GDN_CONV - 1):
            back = GDN_CONV - 1 - j
            y = y + prev_ref[pl.ds(SUBLANES + r0 - back, n), :] * cw[j:j + 1]
        if r0 + n == tm:
            tail = qkv[n - SUBLANES:]
            prev_ref[:SUBLANES, :] = tail
            cstate_ref[0] = tail
        y = _silu(y)
        for hd in range(GDN_HEADS):
            for base, ref in ((0, qg_ref), (GDN_QK, kg_ref)):
                z = y[:, base + hd * GDN_DK: base + (hd + 1) * GDN_DK]
                z = z * lax.rsqrt(jnp.sum(z * z, axis=-1, keepdims=True) + L2_EPS)
                ref[0, rows, hd * GDN_DK:(hd + 1) * GDN_DK] = z.astype(BF16)
        vg_ref[0, rows] = y[:, 2 * GDN_QK:].astype(BF16)
        yield
        hl = proj(COL_KB, D_IN_PAD)
        yield
        gate_ref[0, rows] = _silu(hm[:, :GDN_V]).astype(BF16)
        scale = DIFF_DH ** -0.5 * math.log2(math.e)
        for hd in range(DIFF_HEADS):
            sl = slice(hd * LANES, (hd + 1) * LANES)
            qd_ref[0, rows, sl] = (rope(hm[:, GDN_V + hd * LANES:GDN_V + (hd + 1) * LANES])
                                   * scale).astype(BF16)
        yield
        for hd in range(DIFF_HEADS):
            sl = slice(hd * LANES, (hd + 1) * LANES)
            r = rope(hl[:, sl])
            kout_ref[0, head_rows(hd), :] = r
            kd_ref[0, rows, sl] = r.astype(BF16)
        vb = hl[:, DIFF_QK:DIFF_QK + DIFF_V]
        for hd in range(DIFF_HEADS):
            vout_ref[0, head_rows(hd), :] = vb[:, hd * DIFF_DV:(hd + 1) * DIFF_DV]
        vd_ref[0, rows] = vb.astype(BF16)
        ab = hl[:, DIFF_QK + DIFF_V:]
        g = -jnp.exp(alog_ref[...]) * _softplus(ab + dtb_ref[...])
        lane = lax.broadcasted_iota(jnp.int32, ab.shape, 1)
        gb_ref[0, rows] = jnp.where(lane < GDN_HEADS, g, _sigmoid(ab))

    n_parts = row_parts if tm % (row_parts * MIN_PART_ROWS) == 0 else 1
    n = tm // n_parts
    waiting = [rows_part(i * n, n) for i in range(n_parts)]
    live = []
    done = object()
    while waiting or live:
        if waiting:
            live.append(waiting.pop(0))
            next(live[-1])
        for g in list(live):
            if next(g, done) is done:
                live.remove(g)


def _inproj(x, w_in_r, conv_w, cbuf8, alog, dtb, cos_t, sin_t, tm):
    B, T, D = x.shape
    nt = T // tm
    tok = lambda width, dt: jax.ShapeDtypeStruct((B, T, width), dt)
    tok_spec = lambda width: pl.BlockSpec((1, tm, width), lambda b, t: (b, t, 0))
    full = lambda a: pl.BlockSpec(a.shape, lambda b, t: (0,) * a.ndim)
    out_shape = (tok(GDN_QK, BF16), tok(GDN_QK, BF16), tok(GDN_V, BF16), tok(GDN_V, BF16),
                 tok(LANES, F32), tok(DIFF_QK, BF16), tok(DIFF_QK, BF16), tok(DIFF_V, BF16),
                 jax.ShapeDtypeStruct((B, T * DIFF_HEADS, 2 * DIFF_DH), F32),
                 jax.ShapeDtypeStruct((B, T * DIFF_HEADS, DIFF_DV), F32),
                 jax.ShapeDtypeStruct((B, SUBLANES, GDN_QKV), F32))
    out_specs = (tok_spec(GDN_QK), tok_spec(GDN_QK), tok_spec(GDN_V), tok_spec(GDN_V),
                 tok_spec(LANES), tok_spec(DIFF_QK), tok_spec(DIFF_QK), tok_spec(DIFF_V),
                 pl.BlockSpec((1, tm * DIFF_HEADS, 2 * DIFF_DH), lambda b, t: (b, t, 0)),
                 pl.BlockSpec((1, tm * DIFF_HEADS, DIFF_DV), lambda b, t: (b, t, 0)),
                 pl.BlockSpec((1, SUBLANES, GDN_QKV), lambda b, t: (b, 0, 0)))
    in_specs = [tok_spec(D), full(w_in_r), full(conv_w),
                pl.BlockSpec((1, SUBLANES, GDN_QKV), lambda b, t: (b, 0, 0)),
                full(alog), full(dtb),
                pl.BlockSpec((tm, LANES), lambda b, t: (t, 0)),
                pl.BlockSpec((tm, LANES), lambda b, t: (t, 0))]
    return pl.pallas_call(
        functools.partial(_inproj_kernel, row_parts=INPROJ_ROW_PARTS),
        grid=(B, nt),
        in_specs=in_specs,
        out_specs=out_specs,
        out_shape=out_shape,
        scratch_shapes=[pltpu.VMEM((SUBLANES + tm, GDN_QKV), F32)],
        compiler_params=pltpu.CompilerParams(
            dimension_semantics=("arbitrary", "arbitrary"),
            vmem_limit_bytes=VMEM_LIMIT_BYTES),
        name="inproj",
    )(x, w_in_r, conv_w, cbuf8, alog, dtb, cos_t, sin_t)


def _split_hi_lo(x):
    hi = x.astype(BF16)
    return hi, (x - hi.astype(F32)).astype(BF16)


def _blockdiag_rows(x, width):
    n = x.shape[1] // width
    blk = lax.broadcasted_iota(jnp.int32, x.shape, 1) // width
    zero = jnp.zeros_like(x)
    return jnp.concatenate([jnp.where(blk == j, x, zero) for j in range(n)], axis=0)


def _packed_dot_split(l2, r2, width):
    l_hi, l_lo = _split_hi_lo(l2)
    bd = _blockdiag_rows(r2.astype(BF16), width)
    return _dot(jnp.concatenate([l_hi, l_lo], axis=1), jnp.concatenate([bd, bd], axis=0))


def _per_head(cols, lane_head):
    out = cols[0]
    for j in range(1, len(cols)):
        out = jnp.where(lane_head >= j, cols[j], out)
    return out


def _gdn_kernel(q_ref, k_ref, v_ref, gate_ref, gb_ref, s0_ref, nw_ref,
                o_ref, sout_ref, s_ref, *, chunk, n_chunks, set_chunks):
    t = pl.program_id(1)

    @pl.when(t == 0)
    def _():
        s_ref[...] = s0_ref[0]

    C = chunk
    G = LANES // C
    W = G * GDN_DK
    n_groups = GDN_HEADS // G
    row = lax.broadcasted_iota(jnp.int32, (C, LANES), 0)
    lane = lax.broadcasted_iota(jnp.int32, (C, LANES), 1)
    lane_in = lane % C
    lane_head_p = lane // C
    lane_head_w = lax.broadcasted_iota(jnp.int32, (C, W), 1) // GDN_DK
    incl = row >= lane_in
    strict = row > lane_in
    eye_b = row == lane_in
    eye = eye_b.astype(F32)
    r1 = lax.broadcasted_iota(jnp.int32, (C, C), 0)
    c1 = lax.broadcasted_iota(jnp.int32, (C, C), 1)
    tri = (r1 >= c1).astype(BF16)
    tri3 = jnp.concatenate([tri, tri, tri], axis=1)
    scale = GDN_DK ** -0.5
    nw = nw_ref[...]
    n_sq = max(int(math.ceil(math.log2(C))) - 1, 0)

    head_slices = [slice(j * GDN_DK, (j + 1) * GDN_DK) for j in range(G)]

    def local_part(chunks, out):
        items = []
        for c in chunks:
            rows = slice(c * C, (c + 1) * C)
            gbc = gb_ref[0, rows, :]
            g_hi = gbc.astype(BF16)
            g_rest = gbc - g_hi.astype(F32)
            g_mid = g_rest.astype(BF16)
            g_lo = (g_rest - g_mid.astype(F32)).astype(BF16)
            gc_all = _dot(tri3, jnp.concatenate([g_hi, g_mid, g_lo], axis=0))
            for gi in range(n_groups):
                heads = [gi * G + j for j in range(G)]
                cols = slice(gi * W, (gi + 1) * W)
                gcols = [gc_all[:, h:h + 1] for h in heads]
                bcols = [gbc[:, GDN_HEADS + h:GDN_HEADS + h + 1] for h in heads]
                gcol_p = _per_head(gcols, lane_head_p)
                grow_p = jnp.sum(jnp.where(eye_b, gcol_p, 0.0), axis=0, keepdims=True)
                decay = jnp.exp(jnp.where(incl, gcol_p - grow_p, -jnp.inf))
                b_w = _per_head(bcols, lane_head_w)
                eg_w = _per_head([jnp.exp(g) for g in gcols], lane_head_w)
                q = q_ref[0, rows, cols].astype(F32) * scale
                k = k_ref[0, rows, cols].astype(F32)
                v = v_ref[0, rows, cols].astype(F32)
                kb = k * b_w
                items.append(dict(c=c, q=q, k=k, kb=kb, vb=v * b_w, kbe=kb * eg_w, qe=q * eg_w,
                                  decay=decay, gcols=gcols))
        yield
        for it in items:
            k_bd = _blockdiag_rows(it["k"].astype(BF16), GDN_DK)
            aq = _nt_dot(jnp.concatenate([it["kb"].astype(BF16), it["q"].astype(BF16)], axis=0),
                         k_bd)
            it["qk"] = aq[C:] * it["decay"]
            it["p"] = -jnp.where(strict, aq[:C] * it["decay"], 0.0)
            it["t"] = eye + it["p"]
        yield
        if n_sq > 0:
            for it in items:
                it["p"] = _packed_dot_split(it["p"], it["p"], C)
            yield
            for _ in range(n_sq - 1):
                for it in items:
                    both = _packed_dot_split(jnp.concatenate([it["t"], it["p"]], axis=0),
                                             it["p"], C)
                    it["t"] = it["t"] + both[:C]
                    it["p"] = both[C:]
                yield
            for it in items:
                it["t"] = it["t"] + _packed_dot_split(it["t"], it["p"], C)
            yield
        for it in items:
            rhs = jnp.concatenate([_blockdiag_rows(it["vb"].astype(BF16), GDN_DV),
                                   _blockdiag_rows(it["kbe"].astype(BF16), GDN_DK)], axis=1)
            uw = _dot(it["t"].astype(BF16), rhs)
            out.setdefault(it["c"], []).append(
                (uw[:, :W], uw[:, W:], it["qk"], it["qe"], it["k"], it["gcols"]))

    state = [s_ref[h] for h in range(GDN_HEADS)]

    def recurrence(chunks, local):
        for c in chunks:
            rows = slice(c * C, (c + 1) * C)
            groups = local[c]
            r = [[_dot(jnp.concatenate([w[:, hs], qe[:, hs]], axis=0).astype(BF16),
                       state[gi * G + j].astype(BF16))
                  for j, hs in enumerate(head_slices)]
                 for gi, (u, w, qk, qe, k, gcols) in enumerate(groups)]
            yield
            v_new_bf = [[(u[:, hs] - r[gi][j][:C]).astype(BF16)
                         for j, hs in enumerate(head_slices)]
                        for gi, (u, w, qk, qe, k, gcols) in enumerate(groups)]
            o = [jnp.concatenate([r[gi][j][C:] for j in range(G)], axis=1)
                 + _dot(qk.astype(BF16),
                        _blockdiag_rows(jnp.concatenate(v_new_bf[gi], axis=1), GDN_DV))
                 for gi, (u, w, qk, qe, k, gcols) in enumerate(groups)]
            for gi, (u, w, qk, qe, k, gcols) in enumerate(groups):
                for j, hs in enumerate(head_slices):
                    h = gi * G + j
                    g_last = gcols[j][C - 1:C, :]
                    k_dec = (k[:, hs] * jnp.exp(g_last - gcols[j])).astype(BF16)
                    state[h] = state[h] * jnp.exp(g_last) + _tn_dot(k_dec, v_new_bf[gi][j])
            yield
            for gi in range(n_groups):
                for j, hs in enumerate(head_slices):
                    h = gi * G + j
                    oh = o[gi][:, hs]
                    on = oh * lax.rsqrt(jnp.mean(oh * oh, axis=-1, keepdims=True) + RMS_EPS) * nw
                    ocols = slice(h * GDN_DV, (h + 1) * GDN_DV)
                    o_ref[0, rows, ocols] = (on * gate_ref[0, rows, ocols].astype(F32)
                                             ).astype(BF16)
            yield

    def alternate(*gens):
        gens = list(gens)
        done = object()
        while gens:
            for g in list(gens):
                if next(g, done) is done:
                    gens.remove(g)

    sets = [list(range(c0, min(c0 + set_chunks, n_chunks))) for c0 in range(0, n_chunks, set_chunks)]
    local = {}
    alternate(local_part(sets[0], local))
    for prev, cur in zip(sets[:-1], sets[1:]):
        alternate(local_part(cur, local), recurrence(prev, local))
    alternate(recurrence(sets[-1], local))

    for h in range(GDN_HEADS):
        s_ref[h] = state[h]
        sout_ref[0, h] = state[h]


def _gdn(qg, kg, vg, gate, gb, s0, norm_w, chunk, n_chunks):
    B, T, _ = qg.shape
    tb = chunk * n_chunks
    nt = T // tb
    tok_spec = lambda width: pl.BlockSpec((1, tb, width), lambda b, t: (b, t, 0))
    st_spec = pl.BlockSpec((1, GDN_HEADS, GDN_DK, GDN_DV), lambda b, t: (b, 0, 0, 0))
    return pl.pallas_call(
        functools.partial(_gdn_kernel, chunk=chunk, n_chunks=n_chunks,
                          set_chunks=GDN_SET_CHUNKS),
        grid=(B, nt),
        in_specs=[tok_spec(GDN_QK), tok_spec(GDN_QK), tok_spec(GDN_V), tok_spec(GDN_V),
                  tok_spec(LANES), st_spec, pl.BlockSpec((1, GDN_DV), lambda b, t: (0, 0))],
        out_specs=(tok_spec(GDN_V), st_spec),
        out_shape=(jax.ShapeDtypeStruct((B, T, GDN_V), BF16),
                   jax.ShapeDtypeStruct((B, GDN_HEADS, GDN_DK, GDN_DV), F32)),
        scratch_shapes=[pltpu.VMEM((GDN_HEADS, GDN_DK, GDN_DV), F32)],
        compiler_params=pltpu.CompilerParams(
            dimension_semantics=("arbitrary", "arbitrary"),
            vmem_limit_bytes=VMEM_LIMIT_BYTES),
        name="gdn",
    )(qg, kg, vg, gate, gb, s0, norm_w)


def _lane_tile(x, width):
    if width % LANES == 0:
        return jnp.concatenate([x] * (width // LANES), axis=1)
    return x[:, :1]


def _attn_kernel(*refs, bq, bk, bk_wide, cache_bk, unroll, q_pos0, k_pos0, lam_init):
    if cache_bk:
        q_ref, k_ref, v_ref, kc_ref, vc_ref, lam_ref, w_ref, o_ref, m_ref, acc_ref = refs
    else:
        q_ref, k_ref, v_ref, lam_ref, w_ref, o_ref, m_ref, acc_ref = refs
    head = pl.program_id(1)
    qi = pl.program_id(2)
    tk = k_ref.shape[1]

    q = q_ref[0]
    lane = lax.broadcasted_iota(jnp.int32, q.shape, 1)
    zero = jnp.zeros_like(q)
    q_maps = [jnp.where(lane < DIFF_DH, q, zero), jnp.where(lane >= DIFF_DH, q, zero)]
    small = bq < LANES
    parts = [(jnp.concatenate(q_maps, axis=0), (0, 1))] if small else [(q_maps[0], (0,)),
                                                                         (q_maps[1], (1,))]

    q_first = q_pos0 + qi * bq
    q_last = q_first + (bq - 1)
    vis_first = (q_first // CHUNK + 1) * CHUNK - k_pos0
    vis_last = (q_last // CHUNK + 1) * CHUNK - k_pos0
    per_wide = bk_wide // bk
    n_wide = jnp.clip(vis_first // bk_wide, 0, tk // bk_wide)
    n_full = jnp.clip(vis_first // bk, 0, tk // bk)
    n_all = jnp.clip((vis_last + bk - 1) // bk, 0, tk // bk)

    m_ref[...] = jnp.full(m_ref.shape, NEG_INF, F32)
    acc_ref[...] = jnp.zeros(acc_ref.shape, F32)

    def load_new(start, width, ref):
        return ref[0, pl.ds(pl.multiple_of(start, bk), width), :].astype(BF16)

    def load_cache(start, width, ref):
        rows = pl.ds(start * DIFF_HEADS + head, width, stride=DIFF_HEADS)
        return ref[0, rows, :].astype(BF16)

    def process(blocks, load, kr, vr):
        scores = [[_nt_dot(qp, load(st, width, kr)) for qp, _ in parts] for st, width, _ in blocks]
        m = [jnp.concatenate([m_ref[mp] for mp in maps], axis=0) for _, maps in parts]
        acc = [jnp.concatenate([acc_ref[mp] for mp in maps], axis=0) for _, maps in parts]
        for (st, width, masked), s_parts in zip(blocks, scores):
            v_blk = load(st, width, vr)
            if not small:
                v_blk = jnp.concatenate([v_blk, jnp.ones((width, LANES), BF16)], axis=1)
            if masked:
                shape = s_parts[0].shape
                kpos = k_pos0 + st + lax.broadcasted_iota(jnp.int32, shape, 1)
                qpos = q_first + lax.broadcasted_iota(jnp.int32, shape, 0) % bq
                visible = kpos // CHUNK <= qpos // CHUNK
            for pi, s in enumerate(s_parts):
                if masked:
                    s = jnp.where(visible, s, NEG_INF)
                m_new = jnp.maximum(m[pi], jnp.max(s, axis=-1, keepdims=True))
                p = jnp.exp2(s - _lane_tile(m_new, width))
                alpha = jnp.exp2(m[pi] - m_new)
                pv = _dot(p.astype(BF16), v_blk)
                if small:
                    row_sum = jnp.sum(p, axis=-1, keepdims=True)
                    pv = jnp.concatenate([pv, jnp.broadcast_to(row_sum, pv.shape)], axis=1)
                acc[pi] = jnp.concatenate([alpha, alpha], axis=1) * acc[pi] + pv
                m[pi] = m_new
        for pi, (_, maps) in enumerate(parts):
            for r, mp in enumerate(maps):
                m_ref[mp] = m[pi][r * bq:(r + 1) * bq]
                acc_ref[mp] = acc[pi][r * bq:(r + 1) * bq]

    def steps(i, carry, masked, n, base, width, load, kr, vr):
        process([(base + (i * n + j) * width, width, masked) for j in range(n)], load, kr, vr)
        return carry

    def unmasked_run(count, base, width, load, kr, vr, max_unroll, min_unroll=1):
        done = 0
        u = max_unroll
        while u >= min_unroll:
            trips = (count - done) // u
            lax.fori_loop(0, trips,
                          functools.partial(steps, masked=False, n=u, base=base + done * width,
                                            width=width, load=load, kr=kr, vr=vr), 0)
            done = done + trips * u
            u //= 2
        return done

    if cache_bk:
        unmasked_run(kc_ref.shape[1] // (cache_bk * DIFF_HEADS), 0, cache_bk, load_cache, kc_ref,
                     vc_ref, unroll)
    wide_done = unmasked_run(n_wide, 0, bk_wide, load_new, k_ref, v_ref, unroll,
                             min(2, unroll))
    rem_wide = n_wide - wide_done
    rem_narrow = n_full - n_wide * per_wide
    n_masked = n_all - n_full
    wide_at = wide_done * bk_wide
    narrow_at = n_wide * bk_wide
    masked_at = n_full * bk
    new_kv = dict(load=load_new, kr=k_ref, vr=v_ref)
    tails = [((1, 1), [(wide_at, bk_wide, False), (narrow_at, bk, False), (masked_at, bk, True)]),
             ((1, 0), [(wide_at, bk_wide, False), (masked_at, bk, True)]),
             ((0, 1), [(narrow_at, bk, False), (masked_at, bk, True)])]
    fused = (n_masked == 1) & (rem_wide + rem_narrow >= 1) & (rem_wide <= 1) & (rem_narrow <= 1)
    for (rw, rn), blocks in tails:
        @pl.when(fused & (rem_wide == rw) & (rem_narrow == rn))
        def _(blocks=blocks):
            process(blocks, **new_kv)
    single = lambda cnt: jnp.where(fused, 0, cnt)
    lax.fori_loop(0, single(rem_wide),
                  functools.partial(steps, masked=False, n=1, base=wide_at, width=bk_wide,
                                    **new_kv), 0)
    lax.fori_loop(0, single(rem_narrow),
                  functools.partial(steps, masked=False, n=1, base=narrow_at, width=bk, **new_kv),
                  0)
    lax.fori_loop(0, single(n_masked),
                  functools.partial(steps, masked=True, n=1, base=masked_at, width=bk, **new_kv), 0)

    lv = lam_ref[...]
    lam = (jnp.exp(jnp.sum(lv[0:1] * lv[1:2], axis=-1, keepdims=True))
           - jnp.exp(jnp.sum(lv[2:3] * lv[3:4], axis=-1, keepdims=True)) + lam_init)
    o = [acc_ref[mp, :, :DIFF_DV] / acc_ref[mp, :, DIFF_DV:] for mp in range(2)]
    out = o[0] - lam * o[1]
    out = out * lax.rsqrt(jnp.mean(out * out, axis=-1, keepdims=True) + RMS_EPS) * w_ref[...]
    o_ref[0] = (out * (1.0 - lam_init)).astype(BF16)


def _attn(qd, k_new, v_new, k_cache, v_cache, lam_p, subln_w, bq, bk, bk_wide, cache_bk, unroll,
          q_pos0, lam_init):
    B, Tq, _ = qd.shape
    Tk = k_new.shape[1]
    nq = Tq // bq
    kv_spec = pl.BlockSpec((1, Tk, LANES), lambda b, h, i: (b, 0, h))
    in_specs = [pl.BlockSpec((1, bq, LANES), lambda b, h, i: (b, i, h)), kv_spec, kv_spec]
    args = [qd, k_new, v_new]
    k_pos0 = 0
    if k_cache is not None:
        cache_spec = pl.BlockSpec((1,) + k_cache.shape[1:], lambda b, h, i: (b, 0, 0))
        in_specs += [cache_spec, cache_spec]
        args += [k_cache, v_cache]
        k_pos0 = k_cache.shape[1] // DIFF_HEADS
    in_specs += [pl.BlockSpec(lam_p.shape, lambda b, h, i: (0, 0)),
                 pl.BlockSpec(subln_w.shape, lambda b, h, i: (0, 0))]
    args += [lam_p, subln_w]
    return pl.pallas_call(
        functools.partial(_attn_kernel, bq=bq, bk=bk, bk_wide=bk_wide,
                          cache_bk=cache_bk if k_cache is not None else 0, unroll=unroll,
                          q_pos0=q_pos0, k_pos0=k_pos0, lam_init=lam_init),
        grid=(B, DIFF_HEADS, nq),
        in_specs=in_specs,
        out_specs=pl.BlockSpec((1, bq, LANES), lambda b, h, i: (b, i, h)),
        out_shape=jax.ShapeDtypeStruct((B, Tq, DIFF_V), BF16),
        scratch_shapes=[pltpu.VMEM((2, bq, LANES), F32),
                        pltpu.VMEM((2, bq, DIFF_DV + LANES), F32)],
        compiler_params=pltpu.CompilerParams(
            dimension_semantics=("arbitrary", "arbitrary", "arbitrary"),
            vmem_limit_bytes=VMEM_LIMIT_BYTES),
        name="diffattn",
    )(*args)


def _ffn_kernel(x_ref, oa_ref, ob_ref, wo_ref, g1_ref, b1_ref, wup_ref, cw_ref, cb_ref, cbuf_ref,
                wdn_ref, g2_ref, b2_ref, y_ref, cstate_ref, prev_ref, *, alpha, d_ff,
                row_parts):
    t = pl.program_id(1)

    @pl.when(t == 0)
    def _():
        prev_ref[...] = cbuf_ref[0]

    tm = x_ref.shape[1]
    tails = {}

    def conv(up, prev, cols):
        cw = cw_ref[:, cols]
        u = up * cw[FFN_CONV - 1:FFN_CONV] + cb_ref[:, cols]
        for j in range(FFN_CONV - 1):
            u = u + _shifted_rows(prev, up, FFN_CONV - 1 - j) * cw[j:j + 1]
        return u

    def rows_part(idx, r0, n):
        rows = slice(r0, r0 + n)
        halves = (slice(0, d_ff), slice(d_ff, 2 * d_ff))
        mix = _dot(oa_ref[0, rows], wo_ref[:GDN_V, :]) + _dot(ob_ref[0, rows], wo_ref[GDN_V:, :])
        yield
        x1 = _layer_norm(alpha * x_ref[0, rows] + mix, g1_ref[...], b1_ref[...])
        x1b = x1.astype(BF16)
        up = [_dot(x1b, wup_ref[:, cols]) for cols in halves]
        tails[idx] = [u_[n - SUBLANES:] for u_ in up]
        yield
        prev = tails[idx - 1] if idx > 0 else [prev_ref[:, cols] for cols in halves]
        ua, ub = [conv(u_, pv, cols) for u_, pv, cols in zip(up, prev, halves)]
        if r0 + n == tm:
            for tail, cols in zip(tails[idx], halves):
                prev_ref[:, cols] = tail
                cstate_ref[0, :, cols] = tail
        hh = (_silu(ua) * ub).astype(BF16)
        yield
        y = _dot(hh, wdn_ref[...])
        yield
        y_ref[0, rows] = _layer_norm(alpha * x1 + y, g2_ref[...], b2_ref[...])

    n_parts = row_parts if tm % (row_parts * MIN_PART_ROWS) == 0 else 1
    n = tm // n_parts
    waiting = [rows_part(i, i * n, n) for i in range(n_parts)]
    live = []
    done = object()
    while waiting or live:
        if waiting:
            live.append(waiting.pop(0))
            next(live[-1])
        for g in list(live):
            if next(g, done) is done:
                live.remove(g)


def _ffn(x, oa, ob, w_o, g1, b1, w_up, cw, cb, cbuf8, w_dn, g2, b2, tm, alpha):
    B, T, D = x.shape
    nt = T // tm
    d_ff = w_dn.shape[0]
    tok_spec = lambda width: pl.BlockSpec((1, tm, width), lambda b, t: (b, t, 0))
    full = lambda a: pl.BlockSpec(a.shape, lambda b, t: (0,) * a.ndim)
    once = lambda a: pl.BlockSpec(a.shape, lambda b, t: (0,) * a.ndim,
                                  pipeline_mode=pl.Buffered(1))
    st_spec = pl.BlockSpec((1, SUBLANES, 2 * d_ff), lambda b, t: (b, 0, 0))
    return pl.pallas_call(
        functools.partial(_ffn_kernel, alpha=alpha, d_ff=d_ff, row_parts=FFN_ROW_PARTS),
        grid=(B, nt),
        in_specs=[tok_spec(D), tok_spec(GDN_V), tok_spec(DIFF_V), once(w_o), full(g1), full(b1),
                  once(w_up), full(cw), full(cb), st_spec, once(w_dn), full(g2), full(b2)],
        out_specs=(tok_spec(D), st_spec),
        out_shape=(jax.ShapeDtypeStruct((B, T, D), F32),
                   jax.ShapeDtypeStruct((B, SUBLANES, 2 * d_ff), F32)),
        scratch_shapes=[pltpu.VMEM((SUBLANES, 2 * d_ff), F32)],
        compiler_params=pltpu.CompilerParams(
            dimension_semantics=("arbitrary", "arbitrary"),
            vmem_limit_bytes=VMEM_LIMIT_BYTES),
        name="ffn",
    )(x, oa, ob, w_o, g1, b1, w_up, cw, cb, cbuf8, w_dn, g2, b2)


def _pad_rows_front(buf, rows):
    B, r, C = buf.shape
    return jnp.concatenate([jnp.zeros((B, rows - r, C), buf.dtype), buf], axis=1)


def _lane_row(vec):
    n = vec.shape[0]
    pad = (-n) % LANES
    return jnp.pad(vec.astype(F32), (0, pad)).reshape(1, n + pad)


def _rope_tables(pos0, T):
    half = DIFF_DH // 2
    inv = ROPE_THETA ** (-jnp.arange(half, dtype=F32) * (2.0 / DIFF_DH))
    pos = pos0 + jnp.arange(T, dtype=jnp.int32)
    ang = pos.astype(F32)[:, None] * inv[None, :]
    cos = jnp.cos(ang)
    sin = jnp.sin(ang)
    reps = LANES // DIFF_DH
    cos_t = jnp.tile(jnp.concatenate([cos, cos], axis=-1), (1, reps))
    sin_t = jnp.tile(jnp.concatenate([-sin, sin], axis=-1), (1, reps))
    return cos_t, sin_t


def _largest_divisor(n, cap, mult):
    best = None
    for d in range(mult, min(n, cap) + 1, mult):
        if n % d == 0:
            best = d
    return best if best is not None else n


def _layer(x, pos0, k_past, v_past, s0, conv_qkv_buf, conv_ffn_buf, p, lam_init, alpha):
    B, T, _ = x.shape
    tm = _largest_divisor(T, 1024, SUBLANES)
    cos_t, sin_t = _rope_tables(pos0, T)
    (qg, kg, vg, gate, gb, qd, kd, vd, k_new, v_new, cq8) = _inproj(
        x, p["w_in"], p["gdn_conv_w"], _pad_rows_front(conv_qkv_buf, SUBLANES),
        p["alog"], p["dtb"], cos_t, sin_t, tm)

    chunk = CHUNK if T % CHUNK == 0 else T
    n_chunks = _largest_divisor(T // chunk, GDN_STEP_CHUNKS, 1)
    oa, s_new = _gdn(qg, kg, vg, gate, gb, s0, p["gdn_norm_w"], chunk, n_chunks)

    if k_past is None:
        k_cache = v_cache = None
        cache_bk = 0
    else:
        k_cache = k_past.reshape(B, -1, 2 * DIFF_DH)
        v_cache = v_past.reshape(B, -1, DIFF_DV)
        cache_bk = _largest_divisor(k_past.shape[1], 1024, 16)
    bq = _largest_divisor(T, 512, 16)
    bk = _largest_divisor(T, 512, 16)
    bk_wide = 2 * bk if T % (2 * bk) == 0 else bk
    ob = _attn(qd, kd, vd, k_cache, v_cache, p["diff_lambda"], p["diff_subln_w"], bq, bk,
               bk_wide, cache_bk, 4, pos0, lam_init)

    tm_f = _largest_divisor(T, 512, SUBLANES)
    y, cf8 = _ffn(x, oa, ob, p["w_o"], p["ln1_g"], p["ln1_b"], p["w_up"], p["ffn_conv_w"],
                  p["ffn_conv_b"], _pad_rows_front(conv_ffn_buf, SUBLANES), p["w_down"],
                  p["ln2_g"], p["ln2_b"], tm_f, alpha)

    new_k = k_new.reshape(B, T, DIFF_HEADS, 2 * DIFF_DH)
    new_v = v_new.reshape(B, T, DIFF_HEADS, DIFF_DV)
    return (y, new_k, new_v, s_new, cq8[:, SUBLANES - (GDN_CONV - 1):],
            cf8[:, SUBLANES - (FFN_CONV - 1):])


def _prep_params(l, w_in, gdn_conv_w, gdn_a_log, gdn_dt_bias, gdn_norm_w, diff_lambda,
                 diff_subln_w, w_o, ln1_g, ln1_b, w_up, ffn_conv_w, ffn_conv_b, w_down,
                 ln2_g, ln2_b):
    o1 = GDN_QKV
    o2 = o1 + GDN_V
    o3 = o2 + GDN_HEADS
    o4 = o3 + GDN_HEADS
    w = w_in[l].astype(BF16)
    d_model = w.shape[0]
    w_r = jnp.concatenate(
        [w[:, :o2], w[:, o4:], w[:, o2:o4],
         jnp.zeros((d_model, LANES - 2 * GDN_HEADS), BF16)], axis=1)
    row = lambda v: v.astype(F32).reshape(1, -1)
    return {
        "w_in": w_r, "gdn_conv_w": gdn_conv_w[l].astype(F32),
        "alog": _lane_row(gdn_a_log[l]), "dtb": _lane_row(gdn_dt_bias[l]),
        "gdn_norm_w": row(gdn_norm_w[l]), "diff_lambda": diff_lambda[l].astype(F32),
        "diff_subln_w": row(diff_subln_w[l]), "w_o": w_o[l].astype(BF16),
        "ln1_g": row(ln1_g[l]), "ln1_b": row(ln1_b[l]), "w_up": w_up[l].astype(BF16),
        "ffn_conv_w": ffn_conv_w[l].astype(F32), "ffn_conv_b": row(ffn_conv_b[l]),
        "w_down": w_down[l].astype(BF16), "ln2_g": row(ln2_g[l]), "ln2_b": row(ln2_b[l]),
    }


def kernel(x_prompt, x_sample, cache_k, cache_v, state_gdn, state_conv_qkv, state_conv_ffn, w_in, gdn_conv_w, gdn_a_log, gdn_dt_bias, gdn_norm_w, diff_lambda, diff_subln_w, w_o, ln1_g, ln1_b, w_up, ffn_conv_w, ffn_conv_b, w_down, ln2_g, ln2_b):
    depth = w_in.shape[0]
    Bp = x_prompt.shape[0]
    past = cache_k.shape[2]
    alpha = (2 * depth) ** 0.25
    xp, xs = x_prompt, x_sample
    outs_p, outs_s = [], []
    for l in range(depth):
        p = _prep_params(l, w_in, gdn_conv_w, gdn_a_log, gdn_dt_bias, gdn_norm_w, diff_lambda,
                         diff_subln_w, w_o, ln1_g, ln1_b, w_up, ffn_conv_w, ffn_conv_b, w_down,
                         ln2_g, ln2_b)
        lam_init = 0.8 - 0.6 * math.exp(-0.3 * l)
        s0_p = jnp.zeros((Bp, GDN_HEADS, GDN_DK, GDN_DV), F32)
        cq0_p = jnp.zeros((Bp, GDN_CONV - 1, GDN_QKV), xp.dtype)
        cf0_p = jnp.zeros((Bp, FFN_CONV - 1, state_conv_ffn.shape[-1]), xp.dtype)
        xp, kp, vp, sp, cqp, cfp = _layer(xp, 0, None, None, s0_p, cq0_p, cf0_p, p, lam_init, alpha)
        xs, ksn, vsn, ssn, cqs, cfs = _layer(xs, past, cache_k[l], cache_v[l], state_gdn[l],
                                             state_conv_qkv[l], state_conv_ffn[l], p, lam_init,
                                             alpha)
        outs_p.append((kp, vp, sp, cqp, cfp))
        outs_s.append((ksn, vsn, ssn, cqs, cfs))

    def stk(outs, i):
        return jnp.stack([o[i] for o in outs])

    return (xp, xs,
            stk(outs_p, 0), stk(outs_p, 1), stk(outs_p, 2), stk(outs_p, 3), stk(outs_p, 4),
            stk(outs_s, 0), stk(outs_s, 1), stk(outs_s, 2), stk(outs_s, 3), stk(outs_s, 4))
```

```python
import functools
import math

import jax
import jax.numpy as jnp
from jax import lax
from jax.experimental import pallas as pl
from jax.experimental.pallas import tpu as pltpu

F32 = jnp.float32
BF16 = jnp.bfloat16

LANES = 128
SUBLANES = 8
VMEM_LIMIT_BYTES = 56 * 1024 * 1024
FFN_ROW_PARTS = 2
MIN_PART_ROWS = 128
INPROJ_ROW_PARTS = 4
GDN_STEP_CHUNKS = 32
GDN_SET_CHUNKS = 4

CHUNK = 64
GDN_HEADS = 4
GDN_DK = 128
GDN_DV = 128
GDN_CONV = 4
DIFF_HEADS = 4
DIFF_DH = 64
DIFF_DV = 2 * DIFF_DH
ROPE_THETA = 10000.0
FFN_CONV = 3
LN_EPS = 1e-5
RMS_EPS = 1e-6
L2_EPS = 1e-6
NEG_INF = -1e30

GDN_QK = GDN_HEADS * GDN_DK
GDN_V = GDN_HEADS * GDN_DV
GDN_QKV = 2 * GDN_QK + GDN_V
DIFF_QK = DIFF_HEADS * 2 * DIFF_DH
DIFF_V = DIFF_HEADS * DIFF_DV

COL_GATE = GDN_QKV
COL_QB = COL_GATE + GDN_V
COL_KB = COL_QB + DIFF_QK
COL_VB = COL_KB + DIFF_QK
COL_AB = COL_VB + DIFF_V
D_IN_PAD = COL_AB + LANES


def _nt_dot(a, b):
    return lax.dot_general(a, b, (((1,), (1,)), ((), ())), preferred_element_type=F32)


def _tn_dot(a, b):
    return lax.dot_general(a, b, (((0,), (0,)), ((), ())), preferred_element_type=F32)


def _dot(a, b):
    return jnp.dot(a, b, preferred_element_type=F32)


def _sigmoid(x):
    return 1.0 / (1.0 + jnp.exp(-x))


def _silu(x):
    return x * _sigmoid(x)


def _softplus(x):
    return jnp.maximum(x, 0.0) + jnp.log(1.0 + jnp.exp(-jnp.abs(x)))


def _layer_norm(x, g, b):
    mu = jnp.mean(x, axis=-1, keepdims=True)
    xc = x - mu
    var = jnp.mean(xc * xc, axis=-1, keepdims=True)
    return xc * lax.rsqrt(var + LN_EPS) * g + b


def _shifted_rows(prev, cur, back):
    if back == 0:
        return cur
    ext = jnp.concatenate([prev, cur], axis=0)
    n = cur.shape[0]
    return ext[SUBLANES - back:SUBLANES - back + n]


def _inproj_kernel(x_ref, w_ref, convw_ref, cbuf_ref, alog_ref, dtb_ref, cos_ref, sin_ref,
                   qg_ref, kg_ref, vg_ref, gate_ref, gb_ref, qd_ref, kd_ref, vd_ref,
                   kout_ref, vout_ref, cstate_ref, prev_ref, *, row_parts):
    t = pl.program_id(1)

    @pl.when(t == 0)
    def _():
        prev_ref[:SUBLANES, :] = cbuf_ref[0]

    tm = x_ref.shape[1]

    def rows_part(r0, n):
        rows = slice(r0, r0 + n)
        x = x_ref[0, rows].astype(BF16)
        proj = lambda lo, hi: _dot(x, w_ref[:, lo:hi])
        cos = cos_ref[rows]
        sin = sin_ref[rows]

        def rope(z):
            first_half = (lax.broadcasted_iota(jnp.int32, z.shape, 1) % DIFF_DH) < (DIFF_DH // 2)
            swapped = jnp.where(first_half, pltpu.roll(z, LANES - DIFF_DH // 2, 1),
                                pltpu.roll(z, DIFF_DH // 2, 1))
            return z * cos + swapped * sin

        head_rows = lambda hd: pl.ds(hd + r0 * DIFF_HEADS, n, stride=DIFF_HEADS)

        qkv = proj(0, COL_GATE)
        yield
        hm = proj(COL_GATE, COL_KB)
        yield
        cw = convw_ref[...]
        prev_ref[SUBLANES + r0:SUBLANES + r0 + n, :] = qkv
        y = qkv * cw[GDN_CONV - 1:GDN_CONV]
        for j in range(GDN_CONV - 1):
            back = GDN_CONV - 1 - j
            y = y + prev_ref[pl.ds(SUBLANES + r0 - back, n), :] * cw[j:j + 1]
        if r0 + n == tm:
            tail = qkv[n - SUBLANES:]
            prev_ref[:SUBLANES, :] = tail
            cstate_ref[0] = tail
        y = _silu(y)
        for hd in range(GDN_HEADS):
            for base, ref in ((0, qg_ref), (GDN_QK, kg_ref)):
                z = y[:, base + hd * GDN_DK: base + (hd + 1) * GDN_DK]
                z = z * lax.rsqrt(jnp.sum(z * z, axis=-1, keepdims=True) + L2_EPS)
                ref[0, rows, hd * GDN_DK:(hd + 1) * GDN_DK] = z.astype(BF16)
        vg_ref[0, rows] = y[:, 2 * GDN_QK:].astype(BF16)
        yield
        hl = proj(COL_KB, D_IN_PAD)
        yield
        gate_ref[0, rows] = _silu(hm[:, :GDN_V]).astype(BF16)
        scale = DIFF_DH ** -0.5 * math.log2(math.e)
        for hd in range(DIFF_HEADS):
            sl = slice(hd * LANES, (hd + 1) * LANES)
            qd_ref[0, rows, sl] = (rope(hm[:, GDN_V + hd * LANES:GDN_V + (hd + 1) * LANES])
                                   * scale).astype(BF16)
        yield
        for hd in range(DIFF_HEADS):
            sl = slice(hd * LANES, (hd + 1) * LANES)
            r = rope(hl[:, sl])
            kout_ref[0, head_rows(hd), :] = r
            kd_ref[0, rows, sl] = r.astype(BF16)
        vb = hl[:, DIFF_QK:DIFF_QK + DIFF_V]
        for hd in range(DIFF_HEADS):
            vout_ref[0, head_rows(hd), :] = vb[:, hd * DIFF_DV:(hd + 1) * DIFF_DV]
        vd_ref[0, rows] = vb.astype(BF16)
        ab = hl[:, DIFF_QK + DIFF_V:]
        g = -jnp.exp(alog_ref[...]) * _softplus(ab + dtb_ref[...])
        lane = lax.broadcasted_iota(jnp.int32, ab.shape, 1)
        gb_ref[0, rows] = jnp.where(lane < GDN_HEADS, g, _sigmoid(ab))

    n_parts = row_parts if tm % (row_parts * MIN_PART_ROWS) == 0 else 1
    n = tm // n_parts
    waiting = [rows_part(i * n, n) for i in range(n_parts)]
    live = []
    done = object()
    while waiting or live:
        if waiting:
            live.append(waiting.pop(0))
            next(live[-1])
        for g in list(live):
            if next(g, done) is done:
                live.remove(g)


def _inproj(x, w_in_r, conv_w, cbuf8, alog, dtb, cos_t, sin_t, tm):
    B, T, D = x.shape
    nt = T // tm
    tok = lambda width, dt: jax.ShapeDtypeStruct((B, T, width), dt)
    tok_spec = lambda width: pl.BlockSpec((1, tm, width), lambda b, t: (b, t, 0))
    full = lambda a: pl.BlockSpec(a.shape, lambda b, t: (0,) * a.ndim)
    out_shape = (tok(GDN_QK, BF16), tok(GDN_QK, BF16), tok(GDN_V, BF16), tok(GDN_V, BF16),
                 tok(LANES, F32), tok(DIFF_QK, BF16), tok(DIFF_QK, BF16), tok(DIFF_V, BF16),
                 jax.ShapeDtypeStruct((B, T * DIFF_HEADS, 2 * DIFF_DH), F32),
                 jax.ShapeDtypeStruct((B, T * DIFF_HEADS, DIFF_DV), F32),
                 jax.ShapeDtypeStruct((B, SUBLANES, GDN_QKV), F32))
    out_specs = (tok_spec(GDN_QK), tok_spec(GDN_QK), tok_spec(GDN_V), tok_spec(GDN_V),
                 tok_spec(LANES), tok_spec(DIFF_QK), tok_spec(DIFF_QK), tok_spec(DIFF_V),
                 pl.BlockSpec((1, tm * DIFF_HEADS, 2 * DIFF_DH), lambda b, t: (b, t, 0)),
                 pl.BlockSpec((1, tm * DIFF_HEADS, DIFF_DV), lambda b, t: (b, t, 0)),
                 pl.BlockSpec((1, SUBLANES, GDN_QKV), lambda b, t: (b, 0, 0)))
    in_specs = [tok_spec(D), full(w_in_r), full(conv_w),
                pl.BlockSpec((1, SUBLANES, GDN_QKV), lambda b, t: (b, 0, 0)),
                full(alog), full(dtb),
                pl.BlockSpec((tm, LANES), lambda b, t: (t, 0)),
                pl.BlockSpec((tm, LANES), lambda b, t: (t, 0))]
    return pl.pallas_call(
        functools.partial(_inproj_kernel, row_parts=INPROJ_ROW_PARTS),
        grid=(B, nt),
        in_specs=in_specs,
        out_specs=out_specs,
        out_shape=out_shape,
        scratch_shapes=[pltpu.VMEM((SUBLANES + tm, GDN_QKV), F32)],
        compiler_params=pltpu.CompilerParams(
            dimension_semantics=("arbitrary", "arbitrary"),
            vmem_limit_bytes=VMEM_LIMIT_BYTES),
        name="inproj",
    )(x, w_in_r, conv_w, cbuf8, alog, dtb, cos_t, sin_t)


def _split_hi_lo(x):
    hi = x.astype(BF16)
    return hi, (x - hi.astype(F32)).astype(BF16)


def _blockdiag_rows(x, width):
    n = x.shape[1] // width
    blk = lax.broadcasted_iota(jnp.int32, x.shape, 1) // width
    zero = jnp.zeros_like(x)
    return jnp.concatenate([jnp.where(blk == j, x, zero) for j in range(n)], axis=0)


def _packed_dot_split(l2, r2, width):
    l_hi, l_lo = _split_hi_lo(l2)
    bd = _blockdiag_rows(r2.astype(BF16), width)
    return _dot(jnp.concatenate([l_hi, l_lo], axis=1), jnp.concatenate([bd, bd], axis=0))


def _per_head(cols, lane_head):
    out = cols[0]
    for j in range(1, len(cols)):
        out = jnp.where(lane_head >= j, cols[j], out)
    return out


def _gdn_kernel(q_ref, k_ref, v_ref, gate_ref, gb_ref, s0_ref, nw_ref,
                o_ref, sout_ref, s_ref, *, chunk, n_chunks, set_chunks):
    t = pl.program_id(1)

    @pl.when(t == 0)
    def _():
        s_ref[...] = s0_ref[0]

    C = chunk
    G = LANES // C
    W = G * GDN_DK
    n_groups = GDN_HEADS // G
    row = lax.broadcasted_iota(jnp.int32, (C, LANES), 0)
    lane = lax.broadcasted_iota(jnp.int32, (C, LANES), 1)
    lane_in = lane % C
    lane_head_p = lane // C
    lane_head_w = lax.broadcasted_iota(jnp.int32, (C, W), 1) // GDN_DK
    incl = row >= lane_in
    strict = row > lane_in
    eye_b = row == lane_in
    eye = eye_b.astype(F32)
    r1 = lax.broadcasted_iota(jnp.int32, (C, C), 0)
    c1 = lax.broadcasted_iota(jnp.int32, (C, C), 1)
    tri = (r1 >= c1).astype(BF16)
    tri3 = jnp.concatenate([tri, tri, tri], axis=1)
    scale = GDN_DK ** -0.5
    nw = nw_ref[...]
    n_sq = max(int(math.ceil(math.log2(C))) - 1, 0)

    head_slices = [slice(j * GDN_DK, (j + 1) * GDN_DK) for j in range(G)]

    def local_part(chunks, out):
        items = []
        for c in chunks:
            rows = slice(c * C, (c + 1) * C)
            gbc = gb_ref[0, rows, :]
            g_hi = gbc.astype(BF16)
            g_rest = gbc - g_hi.astype(F32)
            g_mid = g_rest.astype(BF16)
            g_lo = (g_rest - g_mid.astype(F32)).astype(BF16)
            gc_all = _dot(tri3, jnp.concatenate([g_hi, g_mid, g_lo], axis=0))
            for gi in range(n_groups):
                heads = [gi * G + j for j in range(G)]
                cols = slice(gi * W, (gi + 1) * W)
                gcols = [gc_all[:, h:h + 1] for h in heads]
                bcols = [gbc[:, GDN_HEADS + h:GDN_HEADS + h + 1] for h in heads]
                gcol_p = _per_head(gcols, lane_head_p)
                grow_p = jnp.sum(jnp.where(eye_b, gcol_p, 0.0), axis=0, keepdims=True)
                decay = jnp.exp(jnp.where(incl, gcol_p - grow_p, -jnp.inf))
                b_w = _per_head(bcols, lane_head_w)
                eg_w = _per_head([jnp.exp(g) for g in gcols], lane_head_w)
                q = q_ref[0, rows, cols].astype(F32) * scale
                k = k_ref[0, rows, cols].astype(F32)
                v = v_ref[0, rows, cols].astype(F32)
                kb = k * b_w
                items.append(dict(c=c, q=q, k=k, kb=kb, vb=v * b_w, kbe=kb * eg_w, qe=q * eg_w,
                                  decay=decay, gcols=gcols))
        yield
        for it in items:
            k_bd = _blockdiag_rows(it["k"].astype(BF16), GDN_DK)
            aq = _nt_dot(jnp.concatenate([it["kb"].astype(BF16), it["q"].astype(BF16)], axis=0),
                         k_bd)
            it["qk"] = aq[C:] * it["decay"]
            it["p"] = -jnp.where(strict, aq[:C] * it["decay"], 0.0)
            it["t"] = eye + it["p"]
        yield
        if n_sq > 0:
            for it in items:
                it["p"] = _packed_dot_split(it["p"], it["p"], C)
            yield
            for _ in range(n_sq - 1):
                for it in items:
                    both = _packed_dot_split(jnp.concatenate([it["t"], it["p"]], axis=0),
                                             it["p"], C)
                    it["t"] = it["t"] + both[:C]
                    it["p"] = both[C:]
                yield
            for it in items:
                it["t"] = it["t"] + _packed_dot_split(it["t"], it["p"], C)
            yield
        for it in items:
            rhs = jnp.concatenate([_blockdiag_rows(it["vb"].astype(BF16), GDN_DV),
                                   _blockdiag_rows(it["kbe"].astype(BF16), GDN_DK)], axis=1)
            uw = _dot(it["t"].astype(BF16), rhs)
            out.setdefault(it["c"], []).append(
                (uw[:, :W], uw[:, W:], it["qk"], it["qe"], it["k"], it["gcols"]))

    state = [s_ref[h] for h in range(GDN_HEADS)]

    def recurrence(chunks, local):
        for c in chunks:
            rows = slice(c * C, (c + 1) * C)
            groups = local[c]
            r = [[_dot(jnp.concatenate([w[:, hs], qe[:, hs]], axis=0).astype(BF16),
                       state[gi * G + j].astype(BF16))
                  for j, hs in enumerate(head_slices)]
                 for gi, (u, w, qk, qe, k, gcols) in enumerate(groups)]
            yield
            v_new_bf = [[(u[:, hs] - r[gi][j][:C]).astype(BF16)
                         for j, hs in enumerate(head_slices)]
                        for gi, (u, w, qk, qe, k, gcols) in enumerate(groups)]
            o = [jnp.concatenate([r[gi][j][C:] for j in range(G)], axis=1)
                 + _dot(qk.astype(BF16),
                        _blockdiag_rows(jnp.concatenate(v_new_bf[gi], axis=1), GDN_DV))
                 for gi, (u, w, qk, qe, k, gcols) in enumerate(groups)]
            for gi, (u, w, qk, qe, k, gcols) in enumerate(groups):
                for j, hs in enumerate(head_slices):
                    h = gi * G + j
                    g_last = gcols[j][C - 1:C, :]
                    k_dec = (k[:, hs] * jnp.exp(g_last - gcols[j])).astype(BF16)
                    state[h] = state[h] * jnp.exp(g_last) + _tn_dot(k_dec, v_new_bf[gi][j])
            yield
            for gi in range(n_groups):
                for j, hs in enumerate(head_slices):
                    h = gi * G + j
                    oh = o[gi][:, hs]
                    on = oh * lax.rsqrt(jnp.mean(oh * oh, axis=-1, keepdims=True) + RMS_EPS) * nw
                    ocols = slice(h * GDN_DV, (h + 1) * GDN_DV)
                    o_ref[0, rows, ocols] = (on * gate_ref[0, rows, ocols].astype(F32)
                                             ).astype(BF16)
            yield

    def alternate(*gens):
        gens = list(gens)
        done = object()
        while gens:
            for g in list(gens):
                if next(g, done) is done:
                    gens.remove(g)

    sets = [list(range(c0, min(c0 + set_chunks, n_chunks))) for c0 in range(0, n_chunks, set_chunks)]
    local = {}
    alternate(local_part(sets[0], local))
    for prev, cur in zip(sets[:-1], sets[1:]):
        alternate(local_part(cur, local), recurrence(prev, local))
    alternate(recurrence(sets[-1], local))

    for h in range(GDN_HEADS):
        s_ref[h] = state[h]
        sout_ref[0, h] = state[h]


def _gdn(qg, kg, vg, gate, gb, s0, norm_w, chunk, n_chunks):
    B, T, _ = qg.shape
    tb = chunk * n_chunks
    nt = T // tb
    tok_spec = lambda width: pl.BlockSpec((1, tb, width), lambda b, t: (b, t, 0))
    st_spec = pl.BlockSpec((1, GDN_HEADS, GDN_DK, GDN_DV), lambda b, t: (b, 0, 0, 0))
    return pl.pallas_call(
        functools.partial(_gdn_kernel, chunk=chunk, n_chunks=n_chunks,
                          set_chunks=GDN_SET_CHUNKS),
        grid=(B, nt),
        in_specs=[tok_spec(GDN_QK), tok_spec(GDN_QK), tok_spec(GDN_V), tok_spec(GDN_V),
                  tok_spec(LANES), st_spec, pl.BlockSpec((1, GDN_DV), lambda b, t: (0, 0))],
        out_specs=(tok_spec(GDN_V), st_spec),
        out_shape=(jax.ShapeDtypeStruct((B, T, GDN_V), BF16),
                   jax.ShapeDtypeStruct((B, GDN_HEADS, GDN_DK, GDN_DV), F32)),
        scratch_shapes=[pltpu.VMEM((GDN_HEADS, GDN_DK, GDN_DV), F32)],
        compiler_params=pltpu.CompilerParams(
            dimension_semantics=("arbitrary", "arbitrary"),
            vmem_limit_bytes=VMEM_LIMIT_BYTES),
        name="gdn",
    )(qg, kg, vg, gate, gb, s0, norm_w)


def _lane_tile(x, width):
    if width % LANES == 0:
        return jnp.concatenate([x] * (width // LANES), axis=1)
    return x[:, :1]


def _attn_kernel(*refs, bq, bk, bk_wide, cache_bk, unroll, q_pos0, k_pos0, lam_init):
    if cache_bk:
        q_ref, k_ref, v_ref, kc_ref, vc_ref, lam_ref, w_ref, o_ref, m_ref, acc_ref = refs
    else:
        q_ref, k_ref, v_ref, lam_ref, w_ref, o_ref, m_ref, acc_ref = refs
    head = pl.program_id(1)
    qi = pl.program_id(2)
    tk = k_ref.shape[1]

    q = q_ref[0]
    lane = lax.broadcasted_iota(jnp.int32, q.shape, 1)
    zero = jnp.zeros_like(q)
    q_maps = [jnp.where(lane < DIFF_DH, q, zero), jnp.where(lane >= DIFF_DH, q, zero)]
    small = bq < LANES
    parts = [(jnp.concatenate(q_maps, axis=0), (0, 1))] if small else [(q_maps[0], (0,)),
                                                                         (q_maps[1], (1,))]

    q_first = q_pos0 + qi * bq
    q_last = q_first + (bq - 1)
    vis_first = (q_first // CHUNK + 1) * CHUNK - k_pos0
    vis_last = (q_last // CHUNK + 1) * CHUNK - k_pos0
    per_wide = bk_wide // bk
    n_wide = jnp.clip(vis_first // bk_wide, 0, tk // bk_wide)
    n_full = jnp.clip(vis_first // bk, 0, tk // bk)
    n_all = jnp.clip((vis_last + bk - 1) // bk, 0, tk // bk)

    m_ref[...] = jnp.full(m_ref.shape, NEG_INF, F32)
    acc_ref[...] = jnp.zeros(acc_ref.shape, F32)

    def load_new(start, width, ref):
        return ref[0, pl.ds(pl.multiple_of(start, bk), width), :].astype(BF16)

    def load_cache(start, width, ref):
        rows = pl.ds(start * DIFF_HEADS + head, width, stride=DIFF_HEADS)
        return ref[0, rows, :].astype(BF16)

    def process(blocks, load, kr, vr):
        scores = [[_nt_dot(qp, load(st, width, kr)) for qp, _ in parts] for st, width, _ in blocks]
        m = [jnp.concatenate([m_ref[mp] for mp in maps], axis=0) for _, maps in parts]
        acc = [jnp.concatenate([acc_ref[mp] for mp in maps], axis=0) for _, maps in parts]
        for (st, width, masked), s_parts in zip(blocks, scores):
            v_blk = load(st, width, vr)
            if not small:
                v_blk = jnp.concatenate([v_blk, jnp.ones((width, LANES), BF16)], axis=1)
            if masked:
                shape = s_parts[0].shape
                kpos = k_pos0 + st + lax.broadcasted_iota(jnp.int32, shape, 1)
                qpos = q_first + lax.broadcasted_iota(jnp.int32, shape, 0) % bq
                visible = kpos // CHUNK <= qpos // CHUNK
            for pi, s in enumerate(s_parts):
                if masked:
                    s = jnp.where(visible, s, NEG_INF)
                m_new = jnp.maximum(m[pi], jnp.max(s, axis=-1, keepdims=True))
                p = jnp.exp2(s - _lane_tile(m_new, width))
                alpha = jnp.exp2(m[pi] - m_new)
                pv = _dot(p.astype(BF16), v_blk)
                if small:
                    row_sum = jnp.sum(p, axis=-1, keepdims=True)
                    pv = jnp.concatenate([pv, jnp.broadcast_to(row_sum, pv.shape)], axis=1)
                acc[pi] = jnp.concatenate([alpha, alpha], axis=1) * acc[pi] + pv
                m[pi] = m_new
        for pi, (_, maps) in enumerate(parts):
            for r, mp in enumerate(maps):
                m_ref[mp] = m[pi][r * bq:(r + 1) * bq]
                acc_ref[mp] = acc[pi][r * bq:(r + 1) * bq]

    def steps(i, carry, masked, n, base, width, load, kr, vr):
        process([(base + (i * n + j) * width, width, masked) for j in range(n)], load, kr, vr)
        return carry

    def unmasked_run(count, base, width, load, kr, vr, max_unroll, min_unroll=1):
        done = 0
        u = max_unroll
        while u >= min_unroll:
            trips = (count - done) // u
            lax.fori_loop(0, trips,
                          functools.partial(steps, masked=False, n=u, base=base + done * width,
                                            width=width, load=load, kr=kr, vr=vr), 0)
            done = done + trips * u
            u //= 2
        return done

    if cache_bk:
        unmasked_run(kc_ref.shape[1] // (cache_bk * DIFF_HEADS), 0, cache_bk, load_cache, kc_ref,
                     vc_ref, unroll)
    wide_done = unmasked_run(n_wide, 0, bk_wide, load_new, k_ref, v_ref, unroll,
                             min(2, unroll))
    rem_wide = n_wide - wide_done
    rem_narrow = n_full - n_wide * per_wide
    n_masked = n_all - n_full
    wide_at = wide_done * bk_wide
    narrow_at = n_wide * bk_wide
    masked_at = n_full * bk
    new_kv = dict(load=load_new, kr=k_ref, vr=v_ref)
    tails = [((1, 1), [(wide_at, bk_wide, False), (narrow_at, bk, False), (masked_at, bk, True)]),
             ((1, 0), [(wide_at, bk_wide, False), (masked_at, bk, True)]),
             ((0, 1), [(narrow_at, bk, False), (masked_at, bk, True)])]
    fused = (n_masked == 1) & (rem_wide + rem_narrow >= 1) & (rem_wide <= 1) & (rem_narrow <= 1)
    for (rw, rn), blocks in tails:
        @pl.when(fused & (rem_wide == rw) & (rem_narrow == rn))
        def _(blocks=blocks):
            process(blocks, **new_kv)
    single = lambda cnt: jnp.where(fused, 0, cnt)
    lax.fori_loop(0, single(rem_wide),
                  functools.partial(steps, masked=False, n=1, base=wide_at, width=bk_wide,
                                    **new_kv), 0)
    lax.fori_loop(0, single(rem_narrow),
                  functools.partial(steps, masked=False, n=1, base=narrow_at, width=bk, **new_kv),
                  0)
    lax.fori_loop(0, single(n_masked),
                  functools.partial(steps, masked=True, n=1, base=masked_at, width=bk, **new_kv), 0)

    lv = lam_ref[...]
    lam = (jnp.exp(jnp.sum(lv[0:1] * lv[1:2], axis=-1, keepdims=True))
           - jnp.exp(jnp.sum(lv[2:3] * lv[3:4], axis=-1, keepdims=True)) + lam_init)
    o = [acc_ref[mp, :, :DIFF_DV] / acc_ref[mp, :, DIFF_DV:] for mp in range(2)]
    out = o[0] - lam * o[1]
    out = out * lax.rsqrt(jnp.mean(out * out, axis=-1, keepdims=True) + RMS_EPS) * w_ref[...]
    o_ref[0] = (out * (1.0 - lam_init)).astype(BF16)


def _attn(qd, k_new, v_new, k_cache, v_cache, lam_p, subln_w, bq, bk, bk_wide, cache_bk, unroll,
          q_pos0, lam_init):
    B, Tq, _ = qd.shape
    Tk = k_new.shape[1]
    nq = Tq // bq
    kv_spec = pl.BlockSpec((1, Tk, LANES), lambda b, h, i: (b, 0, h))
    in_specs = [pl.BlockSpec((1, bq, LANES), lambda b, h, i: (b, i, h)), kv_spec, kv_spec]
    args = [qd, k_new, v_new]
    k_pos0 = 0
    if k_cache is not None:
        cache_spec = pl.BlockSpec((1,) + k_cache.shape[1:], lambda b, h, i: (b, 0, 0))
        in_specs += [cache_spec, cache_spec]
        args += [k_cache, v_cache]
        k_pos0 = k_cache.shape[1] // DIFF_HEADS
    in_specs += [pl.BlockSpec(lam_p.shape, lambda b, h, i: (0, 0)),
                 pl.BlockSpec(subln_w.shape, lambda b, h, i: (0, 0))]
    args += [lam_p, subln_w]
    return pl.pallas_call(
        functools.partial(_attn_kernel, bq=bq, bk=bk, bk_wide=bk_wide,
                          cache_bk=cache_bk if k_cache is not None else 0, unroll=unroll,
                          q_pos0=q_pos0, k_pos0=k_pos0, lam_init=lam_init),
        grid=(B, DIFF_HEADS, nq),
        in_specs=in_specs,
        out_specs=pl.BlockSpec((1, bq, LANES), lambda b, h, i: (b, i, h)),
        out_shape=jax.ShapeDtypeStruct((B, Tq, DIFF_V), BF16),
        scratch_shapes=[pltpu.VMEM((2, bq, LANES), F32),
                        pltpu.VMEM((2, bq, DIFF_DV + LANES), F32)],
        compiler_params=pltpu.CompilerParams(
            dimension_semantics=("arbitrary", "arbitrary", "arbitrary"),
            vmem_limit_bytes=VMEM_LIMIT_BYTES),
        name="diffattn",
    )(*args)


def _ffn_kernel(x_ref, oa_ref, ob_ref, wo_ref, g1_ref, b1_ref, wup_ref, cw_ref, cb_ref, cbuf_ref,
                wdn_ref, g2_ref, b2_ref, y_ref, cstate_ref, prev_ref, *, alpha, d_ff,
                row_parts):
    t = pl.program_id(1)

    @pl.when(t == 0)
    def _():
        prev_ref[...] = cbuf_ref[0]

    tm = x_ref.shape[1]
    tails = {}

    def conv(up, prev, cols):
        cw = cw_ref[:, cols]
        u = up * cw[FFN_CONV - 1:FFN_CONV] + cb_ref[:, cols]
        for j in range(FFN_CONV - 1):
            u = u + _shifted_rows(prev, up, FFN_CONV - 1 - j) * cw[j:j + 1]
        return u

    def rows_part(idx, r0, n):
        rows = slice(r0, r0 + n)
        halves = (slice(0, d_ff), slice(d_ff, 2 * d_ff))
        mix = _dot(oa_ref[0, rows], wo_ref[:GDN_V, :]) + _dot(ob_ref[0, rows], wo_ref[GDN_V:, :])
        yield
        x1 = _layer_norm(alpha * x_ref[0, rows] + mix, g1_ref[...], b1_ref[...])
        x1b = x1.astype(BF16)
        up = []
        for cols in halves:
            up.append(_dot(x1b, wup_ref[:, cols]))
            yield
        tails[idx] = [u_[n - SUBLANES:] for u_ in up]
        prev = tails[idx - 1] if idx > 0 else [prev_ref[:, cols] for cols in halves]
        ua, ub = [conv(u_, pv, cols) for u_, pv, cols in zip(up, prev, halves)]
        if r0 + n == tm:
            for tail, cols in zip(tails[idx], halves):
                prev_ref[:, cols] = tail
                cstate_ref[0, :, cols] = tail
        hh = (_silu(ua) * ub).astype(BF16)
        yield
        y = _dot(hh, wdn_ref[...])
        yield
        y_ref[0, rows] = _layer_norm(alpha * x1 + y, g2_ref[...], b2_ref[...])

    n_parts = row_parts if tm % (row_parts * MIN_PART_ROWS) == 0 else 1
    n = tm // n_parts
    waiting = [rows_part(i, i * n, n) for i in range(n_parts)]
    live = []
    done = object()
    while waiting or live:
        if waiting:
            live.append(waiting.pop(0))
            next(live[-1])
        for g in list(live):
            if next(g, done) is done:
                live.remove(g)


def _ffn(x, oa, ob, w_o, g1, b1, w_up, cw, cb, cbuf8, w_dn, g2, b2, tm, alpha):
    B, T, D = x.shape
    nt = T // tm
    d_ff = w_dn.shape[0]
    tok_spec = lambda width: pl.BlockSpec((1, tm, width), lambda b, t: (b, t, 0))
    full = lambda a: pl.BlockSpec(a.shape, lambda b, t: (0,) * a.ndim)
    once = lambda a: pl.BlockSpec(a.shape, lambda b, t: (0,) * a.ndim,
                                  pipeline_mode=pl.Buffered(1))
    st_spec = pl.BlockSpec((1, SUBLANES, 2 * d_ff), lambda b, t: (b, 0, 0))
    return pl.pallas_call(
        functools.partial(_ffn_kernel, alpha=alpha, d_ff=d_ff, row_parts=FFN_ROW_PARTS),
        grid=(B, nt),
        in_specs=[tok_spec(D), tok_spec(GDN_V), tok_spec(DIFF_V), once(w_o), full(g1), full(b1),
                  once(w_up), full(cw), full(cb), st_spec, once(w_dn), full(g2), full(b2)],
        out_specs=(tok_spec(D), st_spec),
        out_shape=(jax.ShapeDtypeStruct((B, T, D), F32),
                   jax.ShapeDtypeStruct((B, SUBLANES, 2 * d_ff), F32)),
        scratch_shapes=[pltpu.VMEM((SUBLANES, 2 * d_ff), F32)],
        compiler_params=pltpu.CompilerParams(
            dimension_semantics=("arbitrary", "arbitrary"),
            vmem_limit_bytes=VMEM_LIMIT_BYTES),
        name="ffn",
    )(x, oa, ob, w_o, g1, b1, w_up, cw, cb, cbuf8, w_dn, g2, b2)


def _pad_rows_front(buf, rows):
    B, r, C = buf.shape
    return jnp.concatenate([jnp.zeros((B, rows - r, C), buf.dtype), buf], axis=1)


def _lane_row(vec):
    n = vec.shape[0]
    pad = (-n) % LANES
    return jnp.pad(vec.astype(F32), (0, pad)).reshape(1, n + pad)


def _rope_tables(pos0, T):
    half = DIFF_DH // 2
    inv = ROPE_THETA ** (-jnp.arange(half, dtype=F32) * (2.0 / DIFF_DH))
    pos = pos0 + jnp.arange(T, dtype=jnp.int32)
    ang = pos.astype(F32)[:, None] * inv[None, :]
    cos = jnp.cos(ang)
    sin = jnp.sin(ang)
    reps = LANES // DIFF_DH
    cos_t = jnp.tile(jnp.concatenate([cos, cos], axis=-1), (1, reps))
    sin_t = jnp.tile(jnp.concatenate([-sin, sin], axis=-1), (1, reps))
    return cos_t, sin_t


def _largest_divisor(n, cap, mult):
    best = None
    for d in range(mult, min(n, cap) + 1, mult):
        if n % d == 0:
            best = d
    return best if best is not None else n


def _layer(x, pos0, k_past, v_past, s0, conv_qkv_buf, conv_ffn_buf, p, lam_init, alpha):
    B, T, _ = x.shape
    tm = _largest_divisor(T, 1024, SUBLANES)
    cos_t, sin_t = _rope_tables(pos0, T)
    (qg, kg, vg, gate, gb, qd, kd, vd, k_new, v_new, cq8) = _inproj(
        x, p["w_in"], p["gdn_conv_w"], _pad_rows_front(conv_qkv_buf, SUBLANES),
        p["alog"], p["dtb"], cos_t, sin_t, tm)

    chunk = CHUNK if T % CHUNK == 0 else T
    n_chunks = _largest_divisor(T // chunk, GDN_STEP_CHUNKS, 1)
    oa, s_new = _gdn(qg, kg, vg, gate, gb, s0, p["gdn_norm_w"], chunk, n_chunks)

    if k_past is None:
        k_cache = v_cache = None
        cache_bk = 0
    else:
        k_cache = k_past.reshape(B, -1, 2 * DIFF_DH)
        v_cache = v_past.reshape(B, -1, DIFF_DV)
        cache_bk = _largest_divisor(k_past.shape[1], 1024, 16)
    bq = _largest_divisor(T, 512, 16)
    bk = _largest_divisor(T, 512, 16)
    bk_wide = 2 * bk if T % (2 * bk) == 0 else bk
    ob = _attn(qd, kd, vd, k_cache, v_cache, p["diff_lambda"], p["diff_subln_w"], bq, bk,
               bk_wide, cache_bk, 4, pos0, lam_init)

    tm_f = _largest_divisor(T, 512, SUBLANES)
    y, cf8 = _ffn(x, oa, ob, p["w_o"], p["ln1_g"], p["ln1_b"], p["w_up"], p["ffn_conv_w"],
                  p["ffn_conv_b"], _pad_rows_front(conv_ffn_buf, SUBLANES), p["w_down"],
                  p["ln2_g"], p["ln2_b"], tm_f, alpha)

    new_k = k_new.reshape(B, T, DIFF_HEADS, 2 * DIFF_DH)
    new_v = v_new.reshape(B, T, DIFF_HEADS, DIFF_DV)
    return (y, new_k, new_v, s_new, cq8[:, SUBLANES - (GDN_CONV - 1):],
            cf8[:, SUBLANES - (FFN_CONV - 1):])


def _prep_params(l, w_in, gdn_conv_w, gdn_a_log, gdn_dt_bias, gdn_norm_w, diff_lambda,
                 diff_subln_w, w_o, ln1_g, ln1_b, w_up, ffn_conv_w, ffn_conv_b, w_down,
                 ln2_g, ln2_b):
    o1 = GDN_QKV
    o2 = o1 + GDN_V
    o3 = o2 + GDN_HEADS
    o4 = o3 + GDN_HEADS
    w = w_in[l].astype(BF16)
    d_model = w.shape[0]
    w_r = jnp.concatenate(
        [w[:, :o2], w[:, o4:], w[:, o2:o4],
         jnp.zeros((d_model, LANES - 2 * GDN_HEADS), BF16)], axis=1)
    row = lambda v: v.astype(F32).reshape(1, -1)
    return {
        "w_in": w_r, "gdn_conv_w": gdn_conv_w[l].astype(F32),
        "alog": _lane_row(gdn_a_log[l]), "dtb": _lane_row(gdn_dt_bias[l]),
        "gdn_norm_w": row(gdn_norm_w[l]), "diff_lambda": diff_lambda[l].astype(F32),
        "diff_subln_w": row(diff_subln_w[l]), "w_o": w_o[l].astype(BF16),
        "ln1_g": row(ln1_g[l]), "ln1_b": row(ln1_b[l]), "w_up": w_up[l].astype(BF16),
        "ffn_conv_w": ffn_conv_w[l].astype(F32), "ffn_conv_b": row(ffn_conv_b[l]),
        "w_down": w_down[l].astype(BF16), "ln2_g": row(ln2_g[l]), "ln2_b": row(ln2_b[l]),
    }


def kernel(x_prompt, x_sample, cache_k, cache_v, state_gdn, state_conv_qkv, state_conv_ffn, w_in, gdn_conv_w, gdn_a_log, gdn_dt_bias, gdn_norm_w, diff_lambda, diff_subln_w, w_o, ln1_g, ln1_b, w_up, ffn_conv_w, ffn_conv_b, w_down, ln2_g, ln2_b):
    depth = w_in.shape[0]
    Bp = x_prompt.shape[0]
    past = cache_k.shape[2]
    alpha = (2 * depth) ** 0.25
    xp, xs = x_prompt, x_sample
    outs_p, outs_s = [], []
    for l in range(depth):
        p = _prep_params(l, w_in, gdn_conv_w, gdn_a_log, gdn_dt_bias, gdn_norm_w, diff_lambda,
                         diff_subln_w, w_o, ln1_g, ln1_b, w_up, ffn_conv_w, ffn_conv_b, w_down,
                         ln2_g, ln2_b)
        lam_init = 0.8 - 0.6 * math.exp(-0.3 * l)
        s0_p = jnp.zeros((Bp, GDN_HEADS, GDN_DK, GDN_DV), F32)
        cq0_p = jnp.zeros((Bp, GDN_CONV - 1, GDN_QKV), xp.dtype)
        cf0_p = jnp.zeros((Bp, FFN_CONV - 1, state_conv_ffn.shape[-1]), xp.dtype)
        xp, kp, vp, sp, cqp, cfp = _layer(xp, 0, None, None, s0_p, cq0_p, cf0_p, p, lam_init, alpha)
        xs, ksn, vsn, ssn, cqs, cfs = _layer(xs, past, cache_k[l], cache_v[l], state_gdn[l],
                                             state_conv_qkv[l], state_conv_ffn[l], p, lam_init,
                                             alpha)
        outs_p.append((kp, vp, sp, cqp, cfp))
        outs_s.append((ksn, vsn, ssn, cqs, cfs))

    def stk(outs, i):
        return jnp.stack([o[i] for o in outs])

    return (xp, xs,
            stk(outs_p, 0), stk(outs_p, 1), stk(outs_p, 2), stk(outs_p, 3), stk(outs_p, 4),
            stk(outs_s, 0), stk(outs_s, 1), stk(outs_s, 2), stk(outs_s, 3), stk(outs_s, 4))
```
